```python
import math
import jax, jax.numpy as jnp
from jax import lax
import numpy as np

D_MODEL = 1024
BATCH = 8
SEQ = 4096
DEPTH = 4

N_MIXERS = 2
SSM_EXPAND = 2
D_INNER = SSM_EXPAND * D_MODEL
HEAD_DIM = 64
N_SSM_HEADS = D_INNER // HEAD_DIM
N_GROUPS = 4
HEADS_PER_GROUP = N_SSM_HEADS // N_GROUPS
D_STATE = 128
SSM_CONV = 4
CONV_DIM = D_INNER + 2 * N_GROUPS * D_STATE
IN_PROJ_DIM = 2 * D_INNER + 2 * N_GROUPS * D_STATE + N_SSM_HEADS
CHUNK = 128
CF_KERNEL = 31
N_MEM = 256
XA_HEADS = 4
XA_HEAD_DIM = D_MODEL // XA_HEADS
D_FF = 2816
FFN_CONV = 3
EPS = 1e-6

kernel_name = "hybrid_ssd_conformer_memxattn_convffn"


def rmsnorm(x, g):
    xf = x.astype(jnp.float32)
    y = xf * lax.rsqrt(jnp.mean(xf * xf, axis=-1, keepdims=True) + EPS)
    return (y * g.astype(jnp.float32)).astype(x.dtype)


def layernorm(x, g, b):
    xf = x.astype(jnp.float32)
    mu = jnp.mean(xf, axis=-1, keepdims=True)
    var = jnp.mean(jnp.square(xf - mu), axis=-1, keepdims=True)
    y = (xf - mu) * lax.rsqrt(var + EPS)
    return (y * g.astype(jnp.float32) + b.astype(jnp.float32)).astype(x.dtype)


def causal_dwconv(x, w, b):
    K, C = w.shape
    y = lax.conv_general_dilated(
        x, w[:, None, :].astype(x.dtype), window_strides=(1,), padding=[(K - 1, 0)],
        dimension_numbers=("NWC", "WIO", "NWC"), feature_group_count=C)
    return y + b.astype(x.dtype)


def ssd_mixer(h, in_w, conv_w, conv_b, dt_bias, A_log, D_skip, norm_g, out_w):
    Bsz, L, _ = h.shape
    nc = L // CHUNK
    G, Hg, P, N, Q = N_GROUPS, HEADS_PER_GROUP, HEAD_DIM, D_STATE, CHUNK
    f32 = jnp.float32
    proj = h @ in_w
    z, xBC, dt_raw = jnp.split(proj, [D_INNER, D_INNER + CONV_DIM], axis=-1)
    xBC = jax.nn.silu(causal_dwconv(xBC, conv_w, conv_b))
    xs, Bm, Cm = jnp.split(xBC, [D_INNER, D_INNER + G * N], axis=-1)
    xs = xs.astype(f32).reshape(Bsz, nc, Q, G, Hg, P)
    Bm = Bm.astype(f32).reshape(Bsz, nc, Q, G, N)
    Cm = Cm.astype(f32).reshape(Bsz, nc, Q, G, N)
    dt = jax.nn.softplus(dt_raw.astype(f32) + dt_bias.astype(f32)).reshape(Bsz, nc, Q, G, Hg)
    A = -jnp.exp(A_log.astype(f32)).reshape(G, Hg)
    cs = jnp.cumsum(dt * A, axis=2)
    xdt = xs * dt[..., None]
    tri = jnp.tril(jnp.ones((Q, Q), dtype=bool))
    seg = cs[:, :, :, None] - cs[:, :, None, :]
    decay = jnp.exp(jnp.where(tri[None, None, :, :, None, None], seg, -jnp.inf))
    CB = jnp.einsum("bcign,bcjgn->bcijg", Cm, Bm)
    y_diag = jnp.einsum("bcijgh,bcjghp->bcighp", CB[..., None] * decay, xdt)
    decay_to_end = jnp.exp(cs[:, :, -1:] - cs)
    states = jnp.einsum("bcjgn,bcjgh,bcjghp->bcghpn", Bm, decay_to_end, xdt)
    chunk_decay = jnp.exp(cs[:, :, -1])

    def step(carry, inp):
        st, dec = inp
        return carry * dec[..., None, None] + st, carry

    h0 = jnp.zeros((Bsz, G, Hg, P, N), f32)
    _, prev = lax.scan(step, h0, (jnp.swapaxes(states, 0, 1), jnp.swapaxes(chunk_decay, 0, 1)))
    prev = jnp.swapaxes(prev, 0, 1)
    y_off = jnp.einsum("bcign,bcghpn,bcigh->bcighp", Cm, prev, jnp.exp(cs))
    y = y_diag + y_off + xs * D_skip.astype(f32).reshape(G, Hg)[:, :, None]
    y = y.reshape(Bsz, L, D_INNER).astype(h.dtype)
    y = rmsnorm(y * jax.nn.silu(z), norm_g)
    return y @ out_w


def conformer_conv(h, pw1_w, pw1_b, dw_w, dw_b, ln_g, ln_b, pw2_w, pw2_b):
    u = h @ pw1_w + pw1_b
    a, gt = jnp.split(u, 2, axis=-1)
    c = causal_dwconv(a * jax.nn.sigmoid(gt), dw_w, dw_b)
    c = jax.nn.silu(layernorm(c, ln_g, ln_b))
    return c @ pw2_w + pw2_b


def mem_xattn(h, mem, mem_g, q_w, kv_w, o_w):
    Bsz, L, _ = h.shape
    m = rmsnorm(mem, mem_g)
    q = (h @ q_w).reshape(Bsz, L, XA_HEADS, XA_HEAD_DIM)
    k, v = jnp.split(m @ kv_w, 2, axis=-1)
    k = k.reshape(Bsz, N_MEM, XA_HEADS, XA_HEAD_DIM)
    v = v.reshape(Bsz, N_MEM, XA_HEADS, XA_HEAD_DIM)
    s = jnp.einsum("blhd,bmhd->bhlm", q, k).astype(jnp.float32) * (XA_HEAD_DIM ** -0.5)
    p = jax.nn.softmax(s, axis=-1).astype(v.dtype)
    o = jnp.einsum("bhlm,bmhd->blhd", p, v).reshape(Bsz, L, D_MODEL)
    return o @ o_w


def conv_ffn(h, in_w, conv_w, conv_b, out_w):
    u = causal_dwconv(h @ in_w, conv_w, conv_b)
    g, v = jnp.split(u, 2, axis=-1)
    return (jax.nn.silu(g) * v) @ out_w


def setup_inputs(seed: int = 0) -> dict:
    key = jax.random.key(seed)
    ks = jax.random.split(key, 32)
    nA = (DEPTH + 1) // 2
    nB = DEPTH // 2
    D = D_MODEL

    def w(k, shape, fan_in):
        return jax.random.normal(k, shape, jnp.float32) * (fan_in ** -0.5)

    def gain(k, shape):
        return 1.0 + 0.02 * jax.random.normal(k, shape, jnp.float32)

    def bias(k, shape):
        return 0.02 * jax.random.normal(k, shape, jnp.float32)

    log_dt = jax.random.uniform(ks[5], (nA, N_SSM_HEADS), jnp.float32, math.log(1e-3), math.log(1e-1))
    dt0 = jnp.exp(log_dt)
    return {
        "x": jax.random.normal(ks[0], (BATCH, SEQ, D), jnp.float32),
        "mem": jax.random.normal(ks[1], (BATCH, N_MEM, D), jnp.float32),
        "norm_g": gain(ks[2], (DEPTH, 6, D)),
        "ssm_in_w": w(ks[3], (nA, D, IN_PROJ_DIM), D),
        "ssm_conv_w": w(ks[4], (nA, SSM_CONV, CONV_DIM), SSM_CONV),
        "ssm_conv_b": bias(ks[6], (nA, CONV_DIM)),
        "ssm_dt_bias": dt0 + jnp.log(-jnp.expm1(-dt0)),
        "ssm_A_log": jnp.log(jax.random.uniform(ks[7], (nA, N_SSM_HEADS), jnp.float32, 1.0, 16.0)),
        "ssm_D": 1.0 + 0.1 * jax.random.normal(ks[8], (nA, N_SSM_HEADS), jnp.float32),
        "ssm_norm_g": gain(ks[9], (nA, D_INNER)),
        "ssm_out_w": w(ks[10], (nA, D_INNER, D), D_INNER),
        "cf_pw1_w": w(ks[11], (nB, D, 2 * D), D),
        "cf_pw1_b": bias(ks[12], (nB, 2 * D)),
        "cf_dw_w": w(ks[13], (nB, CF_KERNEL, D), CF_KERNEL),
        "cf_dw_b": bias(ks[14], (nB, D)),
        "cf_ln_g": gain(ks[15], (nB, D)),
        "cf_ln_b": bias(ks[16], (nB, D)),
        "cf_pw2_w": w(ks[17], (nB, D, D), D),
        "cf_pw2_b": bias(ks[18], (nB, D)),
        "xa_mem_g": gain(ks[19], (DEPTH, D)),
        "xa_q_w": w(ks[20], (DEPTH, D, D), D),
        "xa_kv_w": w(ks[21], (DEPTH, D, 2 * D), D),
        "xa_o_w": w(ks[22], (DEPTH, D, D), D),
        "ffn_in_w": w(ks[23], (DEPTH, D, 2 * D_FF), D),
        "ffn_conv_w": w(ks[24], (DEPTH, FFN_CONV, 2 * D_FF), FFN_CONV),
        "ffn_conv_b": bias(ks[25], (DEPTH, 2 * D_FF)),
        "ffn_out_w": w(ks[26], (DEPTH, D_FF, D), D_FF),
    }


def reference(x, mem, norm_g,
              ssm_in_w, ssm_conv_w, ssm_conv_b, ssm_dt_bias, ssm_A_log, ssm_D, ssm_norm_g, ssm_out_w,
              cf_pw1_w, cf_pw1_b, cf_dw_w, cf_dw_b, cf_ln_g, cf_ln_b, cf_pw2_w, cf_pw2_b,
              xa_mem_g, xa_q_w, xa_kv_w, xa_o_w,
              ffn_in_w, ffn_conv_w, ffn_conv_b, ffn_out_w):
    for i in range(DEPTH):
        g = norm_g[i]
        j = i // N_MIXERS
        h = rmsnorm(x, g[0])
        if i % N_MIXERS == 0:
            mix = ssd_mixer(h, ssm_in_w[j], ssm_conv_w[j], ssm_conv_b[j], ssm_dt_bias[j],
                            ssm_A_log[j], ssm_D[j], ssm_norm_g[j], ssm_out_w[j])
        else:
            mix = conformer_conv(h, cf_pw1_w[j], cf_pw1_b[j], cf_dw_w[j], cf_dw_b[j],
                                 cf_ln_g[j], cf_ln_b[j], cf_pw2_w[j], cf_pw2_b[j])
        x = x + rmsnorm(mix, g[1])
        a = mem_xattn(rmsnorm(x, g[2]), mem, xa_mem_g[i], xa_q_w[i], xa_kv_w[i], xa_o_w[i])
        x = x + rmsnorm(a, g[3])
        f = conv_ffn(rmsnorm(x, g[4]), ffn_in_w[i], ffn_conv_w[i], ffn_conv_b[i], ffn_out_w[i])
        x = x + rmsnorm(f, g[5])
    return x
```

```python
import functools

import jax
import jax.numpy as jnp
from jax import lax
from jax.experimental import pallas as pl
from jax.experimental.pallas import tpu as pltpu

F32 = jnp.float32
BF16 = jnp.bfloat16

EPS = 1e-6
HEAD_DIM = 64
N_GROUPS = 4
D_STATE = 128
CHUNK = 128
XA_HEADS = 4
SUBLANES = 8
LANES = 128
CF_HALO = 32
VMEM_LIMIT_BYTES = 56 * 1024 * 1024


def _cparams(*sem):
    return pltpu.CompilerParams(dimension_semantics=sem, vmem_limit_bytes=VMEM_LIMIT_BYTES)


def _const_spec(shape):
    nd = len(shape)
    return pl.BlockSpec(shape, lambda *_: (0,) * nd, pipeline_mode=pl.Buffered(1))


def _rms(xf, g):
    ms = jnp.mean(xf * xf, axis=-1, keepdims=True)
    return xf * lax.rsqrt(ms + EPS) * g


def _dot(a, b):
    return jnp.dot(a, b, preferred_element_type=F32)


def _dot_nt(a, b):
    return lax.dot_general(a, b, (((1,), (1,)), ((), ())), preferred_element_type=F32)


def _sigmoid(x):
    return 1.0 / (1.0 + jnp.exp(-x))


def _shift_rows(u, prev, s, row8):
    r = pltpu.roll(u, s, 0)
    p = pltpu.roll(prev, s, 0)
    top = jnp.where(row8 < s, p, r[:SUBLANES])
    return jnp.concatenate([top, r[SUBLANES:]], axis=0)


def _split_bf16(x, parts):
    out = []
    r = x
    for i in range(parts):
        p = r.astype(BF16)
        out.append(p)
        if i + 1 < parts:
            r = r - p.astype(F32)
    return out


def _kv_kernel(mem_ref, g_ref, w_ref, o_ref):
    m = _rms(mem_ref[0], g_ref[0]).astype(BF16)
    o_ref[0, 0] = _dot(m, w_ref[0]).astype(BF16)


def _memory_kv(mem, mem_g, kv_w):
    depth, d, d2 = kv_w.shape
    b, n_mem, _ = mem.shape
    return pl.pallas_call(
        _kv_kernel,
        grid=(depth, b),
        in_specs=[
            pl.BlockSpec((1, n_mem, d), lambda l, i: (i, 0, 0)),
            pl.BlockSpec((1, 1, d), lambda l, i: (l, 0, 0)),
            pl.BlockSpec((1, d, d2), lambda l, i: (l, 0, 0)),
        ],
        out_specs=pl.BlockSpec((1, 1, n_mem, d2), lambda l, i: (l, i, 0, 0)),
        out_shape=jax.ShapeDtypeStruct((depth, b, n_mem, d2), BF16),
        compiler_params=_cparams("arbitrary", "arbitrary"),
        name="memory_kv",
    )(mem, mem_g.reshape(depth, 1, d), kv_w)


def _xattn_kernel(x_ref, kv_ref, g_ref, qw_ref, ow_ref, o_ref):
    x = x_ref[0]
    d = x.shape[-1]
    dh = d // XA_HEADS
    h = _rms(x, g_ref[2:3, :]).astype(BF16)
    q = (_dot(h, qw_ref[...]) * (dh ** -0.5)).astype(BF16)
    heads = []
    for hd in range(XA_HEADS):
        qh = q[:, hd * dh:(hd + 1) * dh]
        kh = kv_ref[0, :, hd * dh:(hd + 1) * dh]
        vh = kv_ref[0, :, d + hd * dh:d + (hd + 1) * dh]
        s = _dot_nt(qh, kh)
        p = jnp.exp(s - jnp.max(s, axis=-1, keepdims=True))
        l = jnp.sum(p, axis=-1, keepdims=True)
        heads.append((_dot(p.astype(BF16), vh) / l).astype(BF16))
    o = jnp.concatenate(heads, axis=-1)
    a = _dot(o, ow_ref[...])
    o_ref[0] = x + _rms(a, g_ref[3:4, :])


def _xattn(x, kv, g, q_w, o_w, tm):
    b, l, d = x.shape
    n_mem = kv.shape[1]
    return pl.pallas_call(
        _xattn_kernel,
        grid=(b, l // tm),
        in_specs=[
            pl.BlockSpec((1, tm, d), lambda i, t: (i, t, 0)),
            pl.BlockSpec((1, n_mem, 2 * d), lambda i, t: (i, 0, 0)),
            _const_spec(g.shape),
            _const_spec(q_w.shape),
            _const_spec(o_w.shape),
        ],
        out_specs=pl.BlockSpec((1, tm, d), lambda i, t: (i, t, 0)),
        out_shape=jax.ShapeDtypeStruct(x.shape, F32),
        compiler_params=_cparams("arbitrary", "arbitrary"),
        name="xattn",
    )(x, kv, g, q_w, o_w)


def _ffn_kernel(x_ref, g_ref, inw_ref, cw_ref, cb_ref, outw_ref, o_ref, carry_ref, *, nb):
    @pl.when(pl.program_id(1) == 0)
    def _():
        carry_ref[...] = jnp.zeros_like(carry_ref)

    x = x_ref[0]
    tm = x.shape[0]
    f = outw_ref.shape[0]
    h = _rms(x, g_ref[4:5, :]).astype(BF16)
    row8 = lax.broadcasted_iota(jnp.int32, (SUBLANES, nb), 0)
    acc = jnp.zeros(x.shape, F32)
    for j in range(f // nb):
        halves = []
        for half in range(2):
            c0 = half * f + j * nb
            u = _dot(h, inw_ref[:, c0:c0 + nb])
            prev = carry_ref[:, c0:c0 + nb]
            carry_ref[:, c0:c0 + nb] = u[tm - SUBLANES:, :]
            u1 = _shift_rows(u, prev, 1, row8)
            u2 = _shift_rows(u, prev, 2, row8)
            halves.append(cw_ref[2:3, c0:c0 + nb] * u + cw_ref[1:2, c0:c0 + nb] * u1
                          + cw_ref[0:1, c0:c0 + nb] * u2 + cb_ref[:, c0:c0 + nb])
        gate, val = halves
        act = (gate * _sigmoid(gate) * val).astype(BF16)
        acc = acc + _dot(act, outw_ref[j * nb:(j + 1) * nb, :])
    o_ref[0] = x + _rms(acc, g_ref[5:6, :])


def _ffn(x, g, in_w, conv_w, conv_b, out_w, tm, nb):
    b, l, d = x.shape
    f = out_w.shape[0]
    return pl.pallas_call(
        functools.partial(_ffn_kernel, nb=nb),
        grid=(b, l // tm),
        in_specs=[
            pl.BlockSpec((1, tm, d), lambda i, t: (i, t, 0)),
            _const_spec(g.shape),
            _const_spec(in_w.shape),
            _const_spec(conv_w.shape),
            _const_spec((1, 2 * f)),
            _const_spec(out_w.shape),
        ],
        out_specs=pl.BlockSpec((1, tm, d), lambda i, t: (i, t, 0)),
        out_shape=jax.ShapeDtypeStruct(x.shape, F32),
        scratch_shapes=[pltpu.VMEM((SUBLANES, 2 * f), F32)],
        compiler_params=_cparams("arbitrary", "arbitrary"),
        name="conv_ffn",
    )(x, g, in_w, conv_w, conv_b.reshape(1, 2 * f), out_w)


def _conformer_kernel(x_ref, g_ref, w1_ref, b1_ref, dww_ref, dwb_ref, lng_ref, lnb_ref,
                      w2_ref, b2_ref, o_ref, buf_ref):
    @pl.when(pl.program_id(1) == 0)
    def _():
        buf_ref[0:CF_HALO, :] = jnp.zeros((CF_HALO, buf_ref.shape[1]), F32)

    x = x_ref[0]
    tm, d = x.shape
    taps = dww_ref.shape[0]
    h = _rms(x, g_ref[0:1, :]).astype(BF16)
    a = _dot(h, w1_ref[:, 0:d]) + b1_ref[:, 0:d]
    gt = _dot(h, w1_ref[:, d:2 * d]) + b1_ref[:, d:2 * d]
    buf_ref[CF_HALO:CF_HALO + tm, :] = a * _sigmoid(gt)
    c = jnp.zeros((tm, d), F32) + dwb_ref[...]
    off = CF_HALO - (taps - 1)
    for k in range(taps):
        c = c + dww_ref[k:k + 1, :] * buf_ref[off + k:off + k + tm, :]
    buf_ref[0:CF_HALO, :] = buf_ref[tm:tm + CF_HALO, :]
    mu = jnp.mean(c, axis=-1, keepdims=True)
    cc = c - mu
    var = jnp.mean(cc * cc, axis=-1, keepdims=True)
    y = cc * lax.rsqrt(var + EPS) * lng_ref[...] + lnb_ref[...]
    y = (y * _sigmoid(y)).astype(BF16)
    out = _dot(y, w2_ref[...]) + b2_ref[...]
    o_ref[0] = x + _rms(out, g_ref[1:2, :])


def _conformer(x, g, pw1_w, pw1_b, dw_w, dw_b, ln_g, ln_b, pw2_w, pw2_b, tm):
    b, l, d = x.shape
    row = lambda v: v.reshape(1, -1)
    args = (x, g, pw1_w, row(pw1_b), dw_w, row(dw_b), row(ln_g), row(ln_b), pw2_w, row(pw2_b))
    return pl.pallas_call(
        _conformer_kernel,
        grid=(b, l // tm),
        in_specs=[pl.BlockSpec((1, tm, d), lambda i, t: (i, t, 0))]
        + [_const_spec(a.shape) for a in args[1:]],
        out_specs=pl.BlockSpec((1, tm, d), lambda i, t: (i, t, 0)),
        out_shape=jax.ShapeDtypeStruct(x.shape, F32),
        scratch_shapes=[pltpu.VMEM((CF_HALO + tm, d), F32)],
        compiler_params=_cparams("arbitrary", "arbitrary"),
        name="conformer",
    )(*args)


def _ssd_inproj_kernel(x_ref, g_ref, w_ref, wdt_ref, z_ref, xbc_ref, dt_ref, *, nb):
    h = _rms(x_ref[...], g_ref[0:1, :]).astype(BF16)
    di = z_ref.shape[1]
    for j in range(di // nb):
        z_ref[:, j * nb:(j + 1) * nb] = _dot(h, w_ref[:, j * nb:(j + 1) * nb]).astype(BF16)
    for j in range(xbc_ref.shape[1] // nb):
        xbc_ref[:, j * nb:(j + 1) * nb] = _dot(
            h, w_ref[:, di + j * nb:di + (j + 1) * nb]).astype(BF16)
    dt_ref[...] = _dot(h, wdt_ref[...])


def _ssd_inproj(x2d, g, w_zx, w_dt, di, tm, nb):
    t, d = x2d.shape
    conv_dim = w_zx.shape[1] - di
    return pl.pallas_call(
        functools.partial(_ssd_inproj_kernel, nb=nb),
        grid=(t // tm,),
        in_specs=[
            pl.BlockSpec((tm, d), lambda i: (i, 0)),
            _const_spec(g.shape),
            _const_spec(w_zx.shape),
            _const_spec(w_dt.shape),
        ],
        out_specs=[
            pl.BlockSpec((tm, di), lambda i: (i, 0)),
            pl.BlockSpec((tm, conv_dim), lambda i: (i, 0)),
            pl.BlockSpec((tm, LANES), lambda i: (i, 0)),
        ],
        out_shape=[
            jax.ShapeDtypeStruct((t, di), BF16),
            jax.ShapeDtypeStruct((t, conv_dim), BF16),
            jax.ShapeDtypeStruct((t, LANES), F32),
        ],
        compiler_params=_cparams("arbitrary"),
        name="ssd_inproj",
    )(x2d, g, w_zx, w_dt)


def _ssd_scan_kernel(xbc_ref, dt_ref, z_ref, cw_ref, cb_ref, dtb_ref, alog_ref, dx_ref, ng_ref,
                     tri_ref, exp_ref, o_ref, carry_ref, st_ref, xc_ref, y_ref):
    @pl.when(pl.program_id(1) == 0)
    def _():
        carry_ref[...] = jnp.zeros_like(carry_ref)
        st_ref[...] = jnp.zeros_like(st_ref)

    q = xbc_ref.shape[1]
    conv_dim = xbc_ref.shape[2]
    di = z_ref.shape[2]
    gw = di // N_GROUPS
    hg = gw // HEAD_DIM
    taps = cw_ref.shape[0]

    cblk = 512
    row8 = lax.broadcasted_iota(jnp.int32, (SUBLANES, cblk), 0)
    for j in range(conv_dim // cblk):
        sl = slice(j * cblk, (j + 1) * cblk)
        u = xbc_ref[0, :, sl].astype(F32)
        prev = carry_ref[:, sl]
        carry_ref[:, sl] = u[q - SUBLANES:, :]
        c = cw_ref[taps - 1:taps, sl] * u + cb_ref[:, sl]
        for s in range(1, taps):
            c = c + cw_ref[taps - 1 - s:taps - s, sl] * _shift_rows(u, prev, s, row8)
        xc_ref[:, sl] = c * _sigmoid(c)

    xdt = dt_ref[0] + dtb_ref[...]
    dt = jnp.maximum(xdt, 0.0) + jnp.log1p(jnp.exp(-jnp.abs(xdt)))
    da = dt * (-jnp.exp(alog_ref[...]))
    tri = tri_ref[...]
    cs = sum(_dot(tri, part) for part in _split_bf16(da, 3))
    cs_t = cs.T
    dt_t = dt.T
    cs_last = cs[q - 1:q, :]
    stack = jnp.concatenate([
        jnp.exp(cs),
        dt * jnp.exp(cs_last - cs),
        jnp.broadcast_to(jnp.exp(cs_last), (SUBLANES, cs.shape[1])),
    ], axis=0)
    ex = sum(_dot(part, exp_ref[...]) for part in _split_bf16(stack, 2))

    lower = lax.broadcasted_iota(jnp.int32, (q, q), 0) >= lax.broadcasted_iota(jnp.int32, (q, q), 1)
    lane = lax.broadcasted_iota(jnp.int32, (q, 2 * HEAD_DIM), 1)
    for g in range(N_GROUPS):
        gs = slice(g * gw, (g + 1) * gw)
        bm = xc_ref[:, di + g * D_STATE:di + (g + 1) * D_STATE]
        cm = xc_ref[:, di + (N_GROUPS + g) * D_STATE:di + (N_GROUPS + g + 1) * D_STATE]
        bb = bm.astype(BF16)
        cb16 = cm.astype(BF16)
        xs = xc_ref[:, gs]
        cbm = _dot_nt(cb16, bb)
        st = st_ref[g]
        y_off = _dot(cb16, st.astype(BF16)) * ex[0:q, gs]
        xw = (xs * ex[q:2 * q, gs]).astype(BF16)
        st_ref[g] = st * ex[2 * q:2 * q + 1, gs] + _dot(bm.T.astype(BF16), xw)
        for k in range(hg // 2):
            ms = []
            for hh in (2 * k, 2 * k + 1):
                hd = g * hg + hh
                seg = cs[:, hd:hd + 1] - cs_t[hd:hd + 1, :]
                dec = jnp.exp(jnp.where(lower, seg, -jnp.inf))
                ms.append((cbm * dec * dt_t[hd:hd + 1, :]).astype(BF16))
            ks = slice(2 * k * HEAD_DIM, (2 * k + 2) * HEAD_DIM)
            x2 = xs[:, ks]
            rhs = jnp.concatenate([jnp.where(lane < HEAD_DIM, x2, 0.0),
                                   jnp.where(lane >= HEAD_DIM, x2, 0.0)], axis=0).astype(BF16)
            y_diag = _dot(jnp.concatenate(ms, axis=1), rhs)
            ys = slice(g * gw + 2 * k * HEAD_DIM, g * gw + (2 * k + 2) * HEAD_DIM)
            y_ref[:, ys] = y_diag + y_off[:, ks] + x2 * dx_ref[:, ys]

    z = z_ref[0].astype(F32)
    o_ref[0] = _rms(y_ref[...] * (z * _sigmoid(z)), ng_ref[...]).astype(BF16)


def _ssd_scan(xbc, dt_raw, z, conv_w, conv_b, dt_bias, a_log, d_skip, norm_g):
    b, l, conv_dim = xbc.shape
    di = z.shape[2]
    n_heads = di // HEAD_DIM
    q = CHUNK
    pad = lambda v: jnp.pad(v, (0, LANES - n_heads)).reshape(1, LANES)
    tri = jnp.tril(jnp.ones((q, q), BF16))
    expand = (jnp.arange(LANES)[:, None] == (jnp.arange(di) // HEAD_DIM)[None, :]).astype(BF16)
    args = (xbc, dt_raw, z, conv_w, conv_b.reshape(1, conv_dim), pad(dt_bias), pad(a_log),
            jnp.repeat(d_skip, HEAD_DIM).reshape(1, di), norm_g.reshape(1, di), tri, expand)
    return pl.pallas_call(
        _ssd_scan_kernel,
        grid=(b, l // q),
        in_specs=[
            pl.BlockSpec((1, q, conv_dim), lambda i, c: (i, c, 0)),
            pl.BlockSpec((1, q, LANES), lambda i, c: (i, c, 0)),
            pl.BlockSpec((1, q, di), lambda i, c: (i, c, 0)),
        ] + [_const_spec(a.shape) for a in args[3:]],
        out_specs=pl.BlockSpec((1, q, di), lambda i, c: (i, c, 0)),
        out_shape=jax.ShapeDtypeStruct((b, l, di), BF16),
        scratch_shapes=[
            pltpu.VMEM((SUBLANES, conv_dim), F32),
            pltpu.VMEM((N_GROUPS, D_STATE, di // N_GROUPS), F32),
            pltpu.VMEM((q, conv_dim), F32),
            pltpu.VMEM((q, di), F32),
        ],
        compiler_params=_cparams("arbitrary", "arbitrary"),
        name="ssd_scan",
    )(*args)


def _proj_residual_kernel(a_ref, x_ref, g_ref, w_ref, o_ref, *, g_row):
    out = _dot(a_ref[...], w_ref[...])
    o_ref[...] = x_ref[...] + _rms(out, g_ref[g_row:g_row + 1, :])


def _proj_residual(a2d, x2d, g, w, g_row, tm):
    t, k = a2d.shape
    d = x2d.shape[1]
    return pl.pallas_call(
        functools.partial(_proj_residual_kernel, g_row=g_row),
        grid=(t // tm,),
        in_specs=[
            pl.BlockSpec((tm, k), lambda i: (i, 0)),
            pl.BlockSpec((tm, d), lambda i: (i, 0)),
            _const_spec(g.shape),
            _const_spec(w.shape),
        ],
        out_specs=pl.BlockSpec((tm, d), lambda i: (i, 0)),
        out_shape=jax.ShapeDtypeStruct(x2d.shape, F32),
        compiler_params=_cparams("arbitrary"),
        name="proj_residual",
    )(a2d, x2d, g, w)


def _tile(n, want):
    t = min(n, want)
    assert n % t == 0, (n, t)
    return t


def kernel(x, mem, norm_g, ssm_in_w, ssm_conv_w, ssm_conv_b, ssm_dt_bias, ssm_A_log, ssm_D, ssm_norm_g, ssm_out_w, cf_pw1_w, cf_pw1_b, cf_dw_w, cf_dw_b, cf_ln_g, cf_ln_b, cf_pw2_w, cf_pw2_b, xa_mem_g, xa_q_w, xa_kv_w, xa_o_w, ffn_in_w, ffn_conv_w, ffn_conv_b, ffn_out_w):
    b, l, d = x.shape
    depth = norm_g.shape[0]
    di = ssm_out_w.shape[1]
    n_heads = ssm_dt_bias.shape[1]
    conv_dim = ssm_conv_w.shape[2]
    assert di == n_heads * HEAD_DIM and conv_dim == di + 2 * N_GROUPS * D_STATE
    assert l % CHUNK == 0 and cf_dw_w.shape[1] - 1 <= CF_HALO
    tm = _tile(l, 512)
    bf = lambda w: w.astype(BF16)

    kv = _memory_kv(mem, xa_mem_g, bf(xa_kv_w))
    for i in range(depth):
        g = norm_g[i]
        j = i // 2
        if i % 2 == 0:
            w_in = ssm_in_w[j]
            w_zx = bf(w_in[:, :di + conv_dim])
            w_dt = bf(jnp.pad(w_in[:, di + conv_dim:], ((0, 0), (0, LANES - n_heads))))
            z, xbc, dt_raw = _ssd_inproj(x.reshape(b * l, d), g, w_zx, w_dt, di, tm, 512)
            yg = _ssd_scan(xbc.reshape(b, l, conv_dim), dt_raw.reshape(b, l, LANES),
                           z.reshape(b, l, di), ssm_conv_w[j], ssm_conv_b[j], ssm_dt_bias[j],
                           ssm_A_log[j], ssm_D[j], ssm_norm_g[j])
            x = _proj_residual(yg.reshape(b * l, di), x.reshape(b * l, d), g, bf(ssm_out_w[j]),
                               1, tm).reshape(b, l, d)
        else:
            x = _conformer(x, g, bf(cf_pw1_w[j]), cf_pw1_b[j], cf_dw_w[j], cf_dw_b[j],
                           cf_ln_g[j], cf_ln_b[j], bf(cf_pw2_w[j]), cf_pw2_b[j], tm)
        x = _xattn(x, kv[i], g, bf(xa_q_w[i]), bf(xa_o_w[i]), tm)
        x = _ffn(x, g, bf(ffn_in_w[i]), ffn_conv_w[i], ffn_conv_b[i], bf(ffn_out_w[i]), tm, 256)
    return x
```

```python
import functools

import jax
import jax.numpy as jnp
from jax import lax
from jax.experimental import pallas as pl
from jax.experimental.pallas import tpu as pltpu

F32 = jnp.float32
BF16 = jnp.bfloat16

EPS = 1e-6
HEAD_DIM = 64
N_GROUPS = 4
D_STATE = 128
CHUNK = 128
XA_HEADS = 4
SUBLANES = 8
LANES = 128
CF_HALO = 32
VMEM_LIMIT_BYTES = 56 * 1024 * 1024


def _cparams(*sem):
    return pltpu.CompilerParams(dimension_semantics=sem, vmem_limit_bytes=VMEM_LIMIT_BYTES)


def _const_spec(shape):
    nd = len(shape)
    return pl.BlockSpec(shape, lambda *_: (0,) * nd, pipeline_mode=pl.Buffered(1))


def _rms(xf, g):
    ms = jnp.mean(xf * xf, axis=-1, keepdims=True)
    return xf * lax.rsqrt(ms + EPS) * g


def _dot(a, b):
    return jnp.dot(a, b, preferred_element_type=F32)


def _dot_nt(a, b):
    return lax.dot_general(a, b, (((1,), (1,)), ((), ())), preferred_element_type=F32)


def _sigmoid(x):
    return 0.5 * jnp.tanh(0.5 * x) + 0.5


def _shift_rows(u, prev, s, row8):
    r = pltpu.roll(u, s, 0)
    p = pltpu.roll(prev, s, 0)
    top = jnp.where(row8 < s, p, r[:SUBLANES])
    return jnp.concatenate([top, r[SUBLANES:]], axis=0)


def _split_bf16(x, parts):
    out = []
    r = x
    for i in range(parts):
        p = r.astype(BF16)
        out.append(p)
        if i + 1 < parts:
            r = r - p.astype(F32)
    return out


def _kv_kernel(mem_ref, g_ref, w_ref, o_ref):
    m = _rms(mem_ref[0], g_ref[0]).astype(BF16)
    o_ref[0, 0] = _dot(m, w_ref[0]).astype(BF16)


def _memory_kv(mem, mem_g, kv_w):
    depth, d, d2 = kv_w.shape
    b, n_mem, _ = mem.shape
    return pl.pallas_call(
        _kv_kernel,
        grid=(depth, b),
        in_specs=[
            pl.BlockSpec((1, n_mem, d), lambda l, i: (i, 0, 0)),
            pl.BlockSpec((1, 1, d), lambda l, i: (l, 0, 0)),
            pl.BlockSpec((1, d, d2), lambda l, i: (l, 0, 0)),
        ],
        out_specs=pl.BlockSpec((1, 1, n_mem, d2), lambda l, i: (l, i, 0, 0)),
        out_shape=jax.ShapeDtypeStruct((depth, b, n_mem, d2), BF16),
        compiler_params=_cparams("arbitrary", "arbitrary"),
        name="memory_kv",
    )(mem, mem_g.reshape(depth, 1, d), kv_w)


def _xattn_kernel(x_ref, kv_ref, g_ref, qw_ref, ow_ref, o_ref):
    x = x_ref[0]
    d = x.shape[-1]
    dh = d // XA_HEADS
    h = _rms(x, g_ref[2:3, :]).astype(BF16)
    q = (_dot(h, qw_ref[...]) * (dh ** -0.5)).astype(BF16)
    heads = []
    for hd in range(XA_HEADS):
        qh = q[:, hd * dh:(hd + 1) * dh]
        kh = kv_ref[0, :, hd * dh:(hd + 1) * dh]
        vh = kv_ref[0, :, d + hd * dh:d + (hd + 1) * dh]
        s = _dot_nt(qh, kh)
        p = jnp.exp(s - jnp.max(s, axis=-1, keepdims=True))
        l = jnp.sum(p, axis=-1, keepdims=True)
        heads.append((_dot(p.astype(BF16), vh) / l).astype(BF16))
    o = jnp.concatenate(heads, axis=-1)
    a = _dot(o, ow_ref[...])
    o_ref[0] = x + _rms(a, g_ref[3:4, :])


def _xattn(x, kv, g, q_w, o_w, tm):
    b, l, d = x.shape
    n_mem = kv.shape[1]
    return pl.pallas_call(
        _xattn_kernel,
        grid=(b, l // tm),
        in_specs=[
            pl.BlockSpec((1, tm, d), lambda i, t: (i, t, 0)),
            pl.BlockSpec((1, n_mem, 2 * d), lambda i, t: (i, 0, 0)),
            _const_spec(g.shape),
            _const_spec(q_w.shape),
            _const_spec(o_w.shape),
        ],
        out_specs=pl.BlockSpec((1, tm, d), lambda i, t: (i, t, 0)),
        out_shape=jax.ShapeDtypeStruct(x.shape, F32),
        compiler_params=_cparams("arbitrary", "arbitrary"),
        name="xattn",
    )(x, kv, g, q_w, o_w)


def _ffn_kernel(x_ref, g_ref, inw_ref, cw_ref, cb_ref, outw_ref, o_ref, carry_ref, *, nb):
    @pl.when(pl.program_id(1) == 0)
    def _():
        carry_ref[...] = jnp.zeros_like(carry_ref)

    x = x_ref[0]
    tm = x.shape[0]
    f = outw_ref.shape[0]
    h = _rms(x, g_ref[4:5, :]).astype(BF16)
    row8 = lax.broadcasted_iota(jnp.int32, (SUBLANES, nb), 0)
    acc = jnp.zeros(x.shape, F32)
    for j in range(f // nb):
        halves = []
        for half in range(2):
            c0 = half * f + j * nb
            u = _dot(h, inw_ref[:, c0:c0 + nb])
            prev = carry_ref[:, c0:c0 + nb]
            carry_ref[:, c0:c0 + nb] = u[tm - SUBLANES:, :]
            u1 = _shift_rows(u, prev, 1, row8)
            u2 = _shift_rows(u, prev, 2, row8)
            halves.append(cw_ref[2:3, c0:c0 + nb] * u + cw_ref[1:2, c0:c0 + nb] * u1
                          + cw_ref[0:1, c0:c0 + nb] * u2 + cb_ref[:, c0:c0 + nb])
        gate, val = halves
        act = (gate * _sigmoid(gate) * val).astype(BF16)
        acc = acc + _dot(act, outw_ref[j * nb:(j + 1) * nb, :])
    o_ref[0] = x + _rms(acc, g_ref[5:6, :])


def _ffn(x, g, in_w, conv_w, conv_b, out_w, tm, nb):
    b, l, d = x.shape
    f = out_w.shape[0]
    return pl.pallas_call(
        functools.partial(_ffn_kernel, nb=nb),
        grid=(b, l // tm),
        in_specs=[
            pl.BlockSpec((1, tm, d), lambda i, t: (i, t, 0)),
            _const_spec(g.shape),
            _const_spec(in_w.shape),
            _const_spec(conv_w.shape),
            _const_spec((1, 2 * f)),
            _const_spec(out_w.shape),
        ],
        out_specs=pl.BlockSpec((1, tm, d), lambda i, t: (i, t, 0)),
        out_shape=jax.ShapeDtypeStruct(x.shape, F32),
        scratch_shapes=[pltpu.VMEM((SUBLANES, 2 * f), F32)],
        compiler_params=_cparams("arbitrary", "arbitrary"),
        name="conv_ffn",
    )(x, g, in_w, conv_w, conv_b.reshape(1, 2 * f), out_w)


def _conformer_kernel(x_ref, g_ref, w1_ref, b1_ref, dww_ref, dwb_ref, lng_ref, lnb_ref,
                      w2_ref, b2_ref, o_ref, buf_ref):
    @pl.when(pl.program_id(1) == 0)
    def _():
        buf_ref[0:CF_HALO, :] = jnp.zeros((CF_HALO, buf_ref.shape[1]), F32)

    x = x_ref[0]
    tm, d = x.shape
    taps = dww_ref.shape[0]
    h = _rms(x, g_ref[0:1, :]).astype(BF16)
    a = _dot(h, w1_ref[:, 0:d]) + b1_ref[:, 0:d]
    gt = _dot(h, w1_ref[:, d:2 * d]) + b1_ref[:, d:2 * d]
    buf_ref[CF_HALO:CF_HALO + tm, :] = a * _sigmoid(gt)
    c = jnp.zeros((tm, d), F32) + dwb_ref[...]
    off = CF_HALO - (taps - 1)
    for k in range(taps):
        c = c + dww_ref[k:k + 1, :] * buf_ref[off + k:off + k + tm, :]
    buf_ref[0:CF_HALO, :] = buf_ref[tm:tm + CF_HALO, :]
    mu = jnp.mean(c, axis=-1, keepdims=True)
    cc = c - mu
    var = jnp.mean(cc * cc, axis=-1, keepdims=True)
    y = cc * lax.rsqrt(var + EPS) * lng_ref[...] + lnb_ref[...]
    y = (y * _sigmoid(y)).astype(BF16)
    out = _dot(y, w2_ref[...]) + b2_ref[...]
    o_ref[0] = x + _rms(out, g_ref[1:2, :])


def _conformer(x, g, pw1_w, pw1_b, dw_w, dw_b, ln_g, ln_b, pw2_w, pw2_b, tm):
    b, l, d = x.shape
    row = lambda v: v.reshape(1, -1)
    args = (x, g, pw1_w, row(pw1_b), dw_w, row(dw_b), row(ln_g), row(ln_b), pw2_w, row(pw2_b))
    return pl.pallas_call(
        _conformer_kernel,
        grid=(b, l // tm),
        in_specs=[pl.BlockSpec((1, tm, d), lambda i, t: (i, t, 0))]
        + [_const_spec(a.shape) for a in args[1:]],
        out_specs=pl.BlockSpec((1, tm, d), lambda i, t: (i, t, 0)),
        out_shape=jax.ShapeDtypeStruct(x.shape, F32),
        scratch_shapes=[pltpu.VMEM((CF_HALO + tm, d), F32)],
        compiler_params=_cparams("arbitrary", "arbitrary"),
        name="conformer",
    )(*args)


def _ssd_inproj_kernel(x_ref, g_ref, w_ref, wdt_ref, z_ref, xbc_ref, dt_ref, *, nb):
    h = _rms(x_ref[...], g_ref[0:1, :]).astype(BF16)
    di = z_ref.shape[1]
    for j in range(di // nb):
        z_ref[:, j * nb:(j + 1) * nb] = _dot(h, w_ref[:, j * nb:(j + 1) * nb]).astype(BF16)
    for j in range(xbc_ref.shape[1] // nb):
        xbc_ref[:, j * nb:(j + 1) * nb] = _dot(
            h, w_ref[:, di + j * nb:di + (j + 1) * nb]).astype(BF16)
    dt_ref[...] = _dot(h, wdt_ref[...])


def _ssd_inproj(x2d, g, w_zx, w_dt, di, tm, nb):
    t, d = x2d.shape
    conv_dim = w_zx.shape[1] - di
    return pl.pallas_call(
        functools.partial(_ssd_inproj_kernel, nb=nb),
        grid=(t // tm,),
        in_specs=[
            pl.BlockSpec((tm, d), lambda i: (i, 0)),
            _const_spec(g.shape),
            _const_spec(w_zx.shape),
            _const_spec(w_dt.shape),
        ],
        out_specs=[
            pl.BlockSpec((tm, di), lambda i: (i, 0)),
            pl.BlockSpec((tm, conv_dim), lambda i: (i, 0)),
            pl.BlockSpec((tm, LANES), lambda i: (i, 0)),
        ],
        out_shape=[
            jax.ShapeDtypeStruct((t, di), BF16),
            jax.ShapeDtypeStruct((t, conv_dim), BF16),
            jax.ShapeDtypeStruct((t, LANES), F32),
        ],
        compiler_params=_cparams("arbitrary"),
        name="ssd_inproj",
    )(x2d, g, w_zx, w_dt)


def _ssd_scan_kernel(xbc_ref, dt_ref, z_ref, cw_ref, cb_ref, dtb_ref, alog_ref, dx_ref, ng_ref,
                     tri_ref, exp_ref, o_ref, carry_ref, st_ref, xc_ref, y_ref):
    @pl.when(pl.program_id(1) == 0)
    def _():
        carry_ref[...] = jnp.zeros_like(carry_ref)
        st_ref[...] = jnp.zeros_like(st_ref)

    q = xbc_ref.shape[1]
    conv_dim = xbc_ref.shape[2]
    di = z_ref.shape[2]
    gw = di // N_GROUPS
    hg = gw // HEAD_DIM
    taps = cw_ref.shape[0]

    cblk = 512
    row8 = lax.broadcasted_iota(jnp.int32, (SUBLANES, cblk), 0)
    for j in range(conv_dim // cblk):
        sl = slice(j * cblk, (j + 1) * cblk)
        u = xbc_ref[0, :, sl].astype(F32)
        prev = carry_ref[:, sl]
        carry_ref[:, sl] = u[q - SUBLANES:, :]
        c = cw_ref[taps - 1:taps, sl] * u + cb_ref[:, sl]
        for s in range(1, taps):
            c = c + cw_ref[taps - 1 - s:taps - s, sl] * _shift_rows(u, prev, s, row8)
        xc_ref[:, sl] = c * _sigmoid(c)

    xdt = dt_ref[0] + dtb_ref[...]
    dt = jnp.maximum(xdt, 0.0) + jnp.log1p(jnp.exp(-jnp.abs(xdt)))
    da = dt * (-jnp.exp(alog_ref[...]))
    tri = tri_ref[...]
    cs = sum(_dot(tri, part) for part in _split_bf16(da, 3))
    cs_t = cs.T
    dt_t = dt.T
    cs_last = cs[q - 1:q, :]
    stack = jnp.concatenate([
        jnp.exp(cs),
        dt * jnp.exp(cs_last - cs),
        jnp.broadcast_to(jnp.exp(cs_last), (SUBLANES, cs.shape[1])),
    ], axis=0)
    ex = sum(_dot(part, exp_ref[...]) for part in _split_bf16(stack, 2))

    lower = lax.broadcasted_iota(jnp.int32, (q, q), 0) >= lax.broadcasted_iota(jnp.int32, (q, q), 1)
    lane = lax.broadcasted_iota(jnp.int32, (q, 2 * HEAD_DIM), 1)
    for g in range(N_GROUPS):
        gs = slice(g * gw, (g + 1) * gw)
        bm = xc_ref[:, di + g * D_STATE:di + (g + 1) * D_STATE]
        cm = xc_ref[:, di + (N_GROUPS + g) * D_STATE:di + (N_GROUPS + g + 1) * D_STATE]
        bb = bm.astype(BF16)
        cb16 = cm.astype(BF16)
        xs = xc_ref[:, gs]
        cbm = _dot_nt(cb16, bb)
        st = st_ref[g]
        y_off = _dot(cb16, st.astype(BF16)) * ex[0:q, gs]
        xw = (xs * ex[q:2 * q, gs]).astype(BF16)
        st_ref[g] = st * ex[2 * q:2 * q + 1, gs] + _dot(bm.T.astype(BF16), xw)
        for k in range(hg // 2):
            ms = []
            for hh in (2 * k, 2 * k + 1):
                hd = g * hg + hh
                seg = cs[:, hd:hd + 1] - cs_t[hd:hd + 1, :]
                dec = jnp.exp(jnp.where(lower, seg, -jnp.inf))
                ms.append((cbm * dec * dt_t[hd:hd + 1, :]).astype(BF16))
            ks = slice(2 * k * HEAD_DIM, (2 * k + 2) * HEAD_DIM)
            x2 = xs[:, ks]
            rhs = jnp.concatenate([jnp.where(lane < HEAD_DIM, x2, 0.0),
                                   jnp.where(lane >= HEAD_DIM, x2, 0.0)], axis=0).astype(BF16)
            y_diag = _dot(jnp.concatenate(ms, axis=1), rhs)
            ys = slice(g * gw + 2 * k * HEAD_DIM, g * gw + (2 * k + 2) * HEAD_DIM)
            y_ref[:, ys] = y_diag + y_off[:, ks] + x2 * dx_ref[:, ys]

    z = z_ref[0].astype(F32)
    o_ref[0] = _rms(y_ref[...] * (z * _sigmoid(z)), ng_ref[...]).astype(BF16)


def _ssd_scan(xbc, dt_raw, z, conv_w, conv_b, dt_bias, a_log, d_skip, norm_g):
    b, l, conv_dim = xbc.shape
    di = z.shape[2]
    n_heads = di // HEAD_DIM
    q = CHUNK
    pad = lambda v: jnp.pad(v, (0, LANES - n_heads)).reshape(1, LANES)
    tri = jnp.tril(jnp.ones((q, q), BF16))
    expand = (jnp.arange(LANES)[:, None] == (jnp.arange(di) // HEAD_DIM)[None, :]).astype(BF16)
    args = (xbc, dt_raw, z, conv_w, conv_b.reshape(1, conv_dim), pad(dt_bias), pad(a_log),
            jnp.repeat(d_skip, HEAD_DIM).reshape(1, di), norm_g.reshape(1, di), tri, expand)
    return pl.pallas_call(
        _ssd_scan_kernel,
        grid=(b, l // q),
        in_specs=[
            pl.BlockSpec((1, q, conv_dim), lambda i, c: (i, c, 0)),
            pl.BlockSpec((1, q, LANES), lambda i, c: (i, c, 0)),
            pl.BlockSpec((1, q, di), lambda i, c: (i, c, 0)),
        ] + [_const_spec(a.shape) for a in args[3:]],
        out_specs=pl.BlockSpec((1, q, di), lambda i, c: (i, c, 0)),
        out_shape=jax.ShapeDtypeStruct((b, l, di), BF16),
        scratch_shapes=[
            pltpu.VMEM((SUBLANES, conv_dim), F32),
            pltpu.VMEM((N_GROUPS, D_STATE, di // N_GROUPS), F32),
            pltpu.VMEM((q, conv_dim), F32),
            pltpu.VMEM((q, di), F32),
        ],
        compiler_params=_cparams("arbitrary", "arbitrary"),
        name="ssd_scan",
    )(*args)


def _proj_residual_kernel(a_ref, x_ref, g_ref, w_ref, o_ref, *, g_row):
    out = _dot(a_ref[...], w_ref[...])
    o_ref[...] = x_ref[...] + _rms(out, g_ref[g_row:g_row + 1, :])


def _proj_residual(a2d, x2d, g, w, g_row, tm):
    t, k = a2d.shape
    d = x2d.shape[1]
    return pl.pallas_call(
        functools.partial(_proj_residual_kernel, g_row=g_row),
        grid=(t // tm,),
        in_specs=[
            pl.BlockSpec((tm, k), lambda i: (i, 0)),
            pl.BlockSpec((tm, d), lambda i: (i, 0)),
            _const_spec(g.shape),
            _const_spec(w.shape),
        ],
        out_specs=pl.BlockSpec((tm, d), lambda i: (i, 0)),
        out_shape=jax.ShapeDtypeStruct(x2d.shape, F32),
        compiler_params=_cparams("arbitrary"),
        name="proj_residual",
    )(a2d, x2d, g, w)


def _tile(n, want):
    t = min(n, want)
    assert n % t == 0, (n, t)
    return t


def kernel(x, mem, norm_g, ssm_in_w, ssm_conv_w, ssm_conv_b, ssm_dt_bias, ssm_A_log, ssm_D, ssm_norm_g, ssm_out_w, cf_pw1_w, cf_pw1_b, cf_dw_w, cf_dw_b, cf_ln_g, cf_ln_b, cf_pw2_w, cf_pw2_b, xa_mem_g, xa_q_w, xa_kv_w, xa_o_w, ffn_in_w, ffn_conv_w, ffn_conv_b, ffn_out_w):
    b, l, d = x.shape
    depth = norm_g.shape[0]
    di = ssm_out_w.shape[1]
    n_heads = ssm_dt_bias.shape[1]
    conv_dim = ssm_conv_w.shape[2]
    assert di == n_heads * HEAD_DIM and conv_dim == di + 2 * N_GROUPS * D_STATE
    assert l % CHUNK == 0 and cf_dw_w.shape[1] - 1 <= CF_HALO
    tm = _tile(l, 512)
    bf = lambda w: w.astype(BF16)

    kv = _memory_kv(mem, xa_mem_g, bf(xa_kv_w))
    for i in range(depth):
        g = norm_g[i]
        j = i // 2
        if i % 2 == 0:
            w_in = ssm_in_w[j]
            w_zx = bf(w_in[:, :di + conv_dim])
            w_dt = bf(jnp.pad(w_in[:, di + conv_dim:], ((0, 0), (0, LANES - n_heads))))
            z, xbc, dt_raw = _ssd_inproj(x.reshape(b * l, d), g, w_zx, w_dt, di, tm, 512)
            yg = _ssd_scan(xbc.reshape(b, l, conv_dim), dt_raw.reshape(b, l, LANES),
                           z.reshape(b, l, di), ssm_conv_w[j], ssm_conv_b[j], ssm_dt_bias[j],
                           ssm_A_log[j], ssm_D[j], ssm_norm_g[j])
            x = _proj_residual(yg.reshape(b * l, di), x.reshape(b * l, d), g, bf(ssm_out_w[j]),
                               1, tm).reshape(b, l, d)
        else:
            x = _conformer(x, g, bf(cf_pw1_w[j]), cf_pw1_b[j], cf_dw_w[j], cf_dw_b[j],
                           cf_ln_g[j], cf_ln_b[j], bf(cf_pw2_w[j]), cf_pw2_b[j], tm)
        x = _xattn(x, kv[i], g, bf(xa_q_w[i]), bf(xa_o_w[i]), tm)
        x = _ffn(x, g, bf(ffn_in_w[i]), ffn_conv_w[i], ffn_conv_b[i], bf(ffn_out_w[i]), tm,
                 ffn_out_w.shape[1])
    return x
```

```python
import functools

import jax
import jax.numpy as jnp
from jax import lax
from jax.experimental import pallas as pl
from jax.experimental.pallas import tpu as pltpu

F32 = jnp.float32
BF16 = jnp.bfloat16

EPS = 1e-6
LOG2E = 1.4426950408889634
HEAD_DIM = 64
N_GROUPS = 4
D_STATE = 128
CHUNK = 128
XA_HEADS = 4
SUBLANES = 8
LANES = 128
CF_ROWS = 128
CF_HALO = 32
VMEM_LIMIT_BYTES = 56 * 1024 * 1024


def _cparams(*sem):
    return pltpu.CompilerParams(dimension_semantics=sem, vmem_limit_bytes=VMEM_LIMIT_BYTES)


def _const_spec(shape):
    nd = len(shape)
    return pl.BlockSpec(shape, lambda *_: (0,) * nd, pipeline_mode=pl.Buffered(1))


def _rms(xf, g):
    ms = jnp.mean(xf * xf, axis=-1, keepdims=True)
    return xf * lax.rsqrt(ms + EPS) * g


def _dot(a, b):
    return jnp.dot(a, b, preferred_element_type=F32)


def _dot_nt(a, b):
    return lax.dot_general(a, b, (((1,), (1,)), ((), ())), preferred_element_type=F32)


def _sigmoid(x):
    return 0.5 * jnp.tanh(0.5 * x) + 0.5


def _silu(x):
    h = 0.5 * x
    return h + h * jnp.tanh(h)


def _shift_rows(u, prev, s, row8):
    r = pltpu.roll(u, s, 0)
    p = pltpu.roll(prev, s, 0)
    top = jnp.where(row8 < s, p, r[:SUBLANES])
    return jnp.concatenate([top, r[SUBLANES:]], axis=0)


def _split_bf16(x, parts):
    out = []
    r = x
    for i in range(parts):
        p = r.astype(BF16)
        out.append(p)
        if i + 1 < parts:
            r = r - p.astype(F32)
    return out


def _kv_kernel(mem_ref, g_ref, w_ref, o_ref):
    m = _rms(mem_ref[0], g_ref[0]).astype(BF16)
    o_ref[0, 0] = _dot(m, w_ref[0]).astype(BF16)


def _memory_kv(mem, mem_g, kv_w):
    depth, d, d2 = kv_w.shape
    b, n_mem, _ = mem.shape
    return pl.pallas_call(
        _kv_kernel,
        grid=(depth, b),
        in_specs=[
            pl.BlockSpec((1, n_mem, d), lambda l, i: (i, 0, 0)),
            pl.BlockSpec((1, 1, d), lambda l, i: (l, 0, 0)),
            pl.BlockSpec((1, d, d2), lambda l, i: (l, 0, 0)),
        ],
        out_specs=pl.BlockSpec((1, 1, n_mem, d2), lambda l, i: (l, i, 0, 0)),
        out_shape=jax.ShapeDtypeStruct((depth, b, n_mem, d2), BF16),
        compiler_params=_cparams("arbitrary", "arbitrary"),
        name="memory_kv",
    )(mem, mem_g.reshape(depth, 1, d), kv_w)


def _xattn_kernel(x_ref, kv_ref, g_ref, qw_ref, ow_ref, o_ref):
    x = x_ref[0]
    d = x.shape[-1]
    dh = d // XA_HEADS
    h = _rms(x, g_ref[2:3, :]).astype(BF16)
    q = (_dot(h, qw_ref[...]) * (dh ** -0.5)).astype(BF16)
    heads = []
    for hd in range(XA_HEADS):
        qh = q[:, hd * dh:(hd + 1) * dh]
        kh = kv_ref[0, :, hd * dh:(hd + 1) * dh]
        vh = kv_ref[0, :, d + hd * dh:d + (hd + 1) * dh]
        s = _dot_nt(qh, kh)
        p = jnp.exp(s - jnp.max(s, axis=-1, keepdims=True))
        l = jnp.sum(p, axis=-1, keepdims=True)
        heads.append((_dot(p.astype(BF16), vh) / l).astype(BF16))
    o = jnp.concatenate(heads, axis=-1)
    a = _dot(o, ow_ref[...])
    o_ref[0] = x + _rms(a, g_ref[3:4, :])


def _xattn(x, kv, g, q_w, o_w, tm):
    b, l, d = x.shape
    n_mem = kv.shape[1]
    return pl.pallas_call(
        _xattn_kernel,
        grid=(b, l // tm),
        in_specs=[
            pl.BlockSpec((1, tm, d), lambda i, t: (i, t, 0)),
            pl.BlockSpec((1, n_mem, 2 * d), lambda i, t: (i, 0, 0)),
            _const_spec(g.shape),
            _const_spec(q_w.shape),
            _const_spec(o_w.shape),
        ],
        out_specs=pl.BlockSpec((1, tm, d), lambda i, t: (i, t, 0)),
        out_shape=jax.ShapeDtypeStruct(x.shape, F32),
        compiler_params=_cparams("arbitrary", "arbitrary"),
        name="xattn",
    )(x, kv, g, q_w, o_w)


def _ffn_kernel(x_ref, g_ref, inw_ref, cw_ref, cb_ref, outw_ref, o_ref, carry_ref, *, nb):
    @pl.when(pl.program_id(1) == 0)
    def _():
        carry_ref[...] = jnp.zeros_like(carry_ref)

    x = x_ref[0]
    tm = x.shape[0]
    f = outw_ref.shape[0]
    h = _rms(x, g_ref[4:5, :]).astype(BF16)
    row8 = lax.broadcasted_iota(jnp.int32, (SUBLANES, nb), 0)
    acc = jnp.zeros(x.shape, F32)
    for j in range(f // nb):
        halves = []
        for half in range(2):
            c0 = half * f + j * nb
            u = _dot(h, inw_ref[:, c0:c0 + nb])
            prev = carry_ref[:, c0:c0 + nb]
            carry_ref[:, c0:c0 + nb] = u[tm - SUBLANES:, :]
            u1 = _shift_rows(u, prev, 1, row8)
            u2 = _shift_rows(u, prev, 2, row8)
            halves.append(cw_ref[2:3, c0:c0 + nb] * u + cw_ref[1:2, c0:c0 + nb] * u1
                          + cw_ref[0:1, c0:c0 + nb] * u2 + cb_ref[:, c0:c0 + nb])
        gate, val = halves
        act = (_silu(gate) * val).astype(BF16)
        acc = acc + _dot(act, outw_ref[j * nb:(j + 1) * nb, :])
    o_ref[0] = x + _rms(acc, g_ref[5:6, :])


def _ffn(x, g, in_w, conv_w, conv_b, out_w, tm, nb):
    b, l, d = x.shape
    f = out_w.shape[0]
    return pl.pallas_call(
        functools.partial(_ffn_kernel, nb=nb),
        grid=(b, l // tm),
        in_specs=[
            pl.BlockSpec((1, tm, d), lambda i, t: (i, t, 0)),
            _const_spec(g.shape),
            _const_spec(in_w.shape),
            _const_spec(conv_w.shape),
            _const_spec((1, 2 * f)),
            _const_spec(out_w.shape),
        ],
        out_specs=pl.BlockSpec((1, tm, d), lambda i, t: (i, t, 0)),
        out_shape=jax.ShapeDtypeStruct(x.shape, F32),
        scratch_shapes=[pltpu.VMEM((SUBLANES, 2 * f), F32)],
        compiler_params=_cparams("arbitrary", "arbitrary"),
        name="conv_ffn",
    )(x, g, in_w, conv_w, conv_b.reshape(1, 2 * f), out_w)


def _conformer_kernel(x_ref, g_ref, w1_ref, b1_ref, dww_ref, dwb_ref, lng_ref, lnb_ref,
                      w2_ref, b2_ref, o_ref, buf_ref, c_ref):
    nslab = buf_ref.shape[0]

    @pl.when(pl.program_id(1) == 0)
    def _():
        buf_ref[:, 0:2 * CF_HALO, :] = jnp.zeros((nslab, 2 * CF_HALO, LANES), F32)

    x = x_ref[0]
    tm, d = x.shape
    taps = dww_ref.shape[1]
    h = _rms(x, g_ref[0:1, :]).astype(BF16)
    a = _dot(h, w1_ref[:, 0:d]) + b1_ref[:, 0:d]
    gt = _dot(h, w1_ref[:, d:2 * d]) + b1_ref[:, d:2 * d]
    glu = a * _sigmoid(gt)
    for j in range(nslab):
        buf_ref[j, pl.ds(2 * CF_HALO, tm, stride=2), :] = glu[:, j * LANES:(j + 1) * LANES]

    nrb = tm // CF_ROWS
    off = CF_HALO - (taps - 1)

    def conv_block(i, carry):
        j = lax.div(i, nrb)
        r0 = lax.rem(i, nrb) * CF_ROWS
        nv = CF_ROWS // SUBLANES
        bias = jnp.broadcast_to(dwb_ref[j], (SUBLANES, LANES))
        acc = [bias] * nv
        for q in range(SUBLANES * (nv - 1) + taps):
            win = buf_ref[j, pl.ds(2 * (r0 + off + q), SUBLANES, stride=2), :]
            for v in range(nv):
                k = q - SUBLANES * v
                if 0 <= k < taps:
                    acc[v] = acc[v] + dww_ref[j, k:k + 1, :] * win
        c_ref[j, pl.ds(pl.multiple_of(r0, CF_ROWS), CF_ROWS), :] = jnp.concatenate(acc, axis=0)
        return carry

    lax.fori_loop(0, nslab * nrb, conv_block, 0)
    for j in range(nslab):
        buf_ref[j, pl.ds(0, CF_HALO, stride=2), :] = buf_ref[j, pl.ds(2 * tm, CF_HALO, stride=2), :]

    c = jnp.concatenate([c_ref[j] for j in range(nslab)], axis=1)
    mu = jnp.mean(c, axis=-1, keepdims=True)
    cc = c - mu
    var = jnp.mean(cc * cc, axis=-1, keepdims=True)
    y = cc * lax.rsqrt(var + EPS) * lng_ref[...] + lnb_ref[...]
    y = _silu(y).astype(BF16)
    out = _dot(y, w2_ref[...]) + b2_ref[...]
    o_ref[0] = x + _rms(out, g_ref[1:2, :])


def _conformer(x, g, pw1_w, pw1_b, dw_w, dw_b, ln_g, ln_b, pw2_w, pw2_b, tm):
    b, l, d = x.shape
    taps = dw_w.shape[0]
    nslab = d // LANES
    row = lambda v: v.reshape(1, -1)
    dww = dw_w.reshape(taps, nslab, LANES).transpose(1, 0, 2)
    dwb = dw_b.reshape(nslab, 1, LANES)
    args = (x, g, pw1_w, row(pw1_b), dww, dwb, row(ln_g), row(ln_b), pw2_w, row(pw2_b))
    return pl.pallas_call(
        _conformer_kernel,
        grid=(b, l // tm),
        in_specs=[pl.BlockSpec((1, tm, d), lambda i, t: (i, t, 0))]
        + [_const_spec(a.shape) for a in args[1:]],
        out_specs=pl.BlockSpec((1, tm, d), lambda i, t: (i, t, 0)),
        out_shape=jax.ShapeDtypeStruct(x.shape, F32),
        scratch_shapes=[pltpu.VMEM((nslab, 2 * (CF_HALO + tm), LANES), F32),
                        pltpu.VMEM((nslab, tm, LANES), F32)],
        compiler_params=_cparams("arbitrary", "arbitrary"),
        name="conformer",
    )(*args)


def _ssd_inproj_kernel(x_ref, g_ref, w_ref, wdt_ref, z_ref, xbc_ref, dt_ref, *, nb):
    h = _rms(x_ref[...], g_ref[0:1, :]).astype(BF16)
    di = z_ref.shape[1]
    for j in range(di // nb):
        z_ref[:, j * nb:(j + 1) * nb] = _dot(h, w_ref[:, j * nb:(j + 1) * nb]).astype(BF16)
    for j in range(xbc_ref.shape[1] // nb):
        xbc_ref[:, j * nb:(j + 1) * nb] = _dot(
            h, w_ref[:, di + j * nb:di + (j + 1) * nb]).astype(BF16)
    dt_ref[...] = _dot(h, wdt_ref[...])


def _ssd_inproj(x2d, g, w_zx, w_dt, di, tm, nb):
    t, d = x2d.shape
    conv_dim = w_zx.shape[1] - di
    return pl.pallas_call(
        functools.partial(_ssd_inproj_kernel, nb=nb),
        grid=(t // tm,),
        in_specs=[
            pl.BlockSpec((tm, d), lambda i: (i, 0)),
            _const_spec(g.shape),
            _const_spec(w_zx.shape),
            _const_spec(w_dt.shape),
        ],
        out_specs=[
            pl.BlockSpec((tm, di), lambda i: (i, 0)),
            pl.BlockSpec((tm, conv_dim), lambda i: (i, 0)),
            pl.BlockSpec((tm, LANES), lambda i: (i, 0)),
        ],
        out_shape=[
            jax.ShapeDtypeStruct((t, di), BF16),
            jax.ShapeDtypeStruct((t, conv_dim), BF16),
            jax.ShapeDtypeStruct((t, LANES), F32),
        ],
        compiler_params=_cparams("arbitrary"),
        name="ssd_inproj",
    )(x2d, g, w_zx, w_dt)


def _ssd_scan_kernel(xbc_ref, dt_ref, z_ref, cw_ref, cb_ref, dtb_ref, alog_ref, dx_ref, ng_ref,
                     tri_ref, exp_ref, o_ref, cbuf_ref, st_ref, xc_ref, y_ref):
    @pl.when(pl.program_id(1) == 0)
    def _():
        cbuf_ref[:, 0:2 * SUBLANES, :] = jnp.zeros((cbuf_ref.shape[0], 2 * SUBLANES, LANES), F32)
        st_ref[...] = jnp.zeros_like(st_ref)

    q = xbc_ref.shape[1]
    conv_dim = xbc_ref.shape[2]
    di = z_ref.shape[2]
    gw = di // N_GROUPS
    hg = gw // HEAD_DIM
    taps = cw_ref.shape[0]

    for j in range(conv_dim // LANES):
        sl = slice(j * LANES, (j + 1) * LANES)
        u = xbc_ref[0, :, sl].astype(F32)
        cbuf_ref[j, pl.ds(2 * SUBLANES, q, stride=2), :] = u
        c = cw_ref[taps - 1:taps, sl] * u + cb_ref[:, sl]
        for s in range(1, taps):
            c = c + cw_ref[taps - 1 - s:taps - s, sl] * cbuf_ref[j, pl.ds(2 * (SUBLANES - s), q, stride=2), :]
        cbuf_ref[j, pl.ds(0, SUBLANES, stride=2), :] = cbuf_ref[j, pl.ds(2 * q, SUBLANES, stride=2), :]
        xc_ref[:, sl] = _silu(c)

    nh = dtb_ref.shape[0]
    xdt = dt_ref[0].T[0:nh, :] + dtb_ref[...]
    dt_t = jnp.maximum(xdt, 0.0) + jnp.log1p(jnp.exp(-jnp.abs(xdt)))
    da_t = dt_t * (-jnp.exp(alog_ref[...]))
    cs_t = sum(_dot(part, tri_ref[...]) for part in _split_bf16(da_t, 3))
    pad_rows = jnp.zeros((LANES - nh, q), F32)
    cs = jnp.concatenate([cs_t, pad_rows], axis=0).T
    dt = jnp.concatenate([dt_t, pad_rows], axis=0).T
    cs2 = cs * LOG2E
    cs2_t = cs_t * LOG2E
    cs_last = cs[q - 1:q, :]
    stack = jnp.concatenate([
        jnp.exp(cs),
        dt * jnp.exp(cs_last - cs),
        jnp.broadcast_to(jnp.exp(cs_last), (SUBLANES, cs.shape[1])),
    ], axis=0)
    ex = sum(_dot(part, exp_ref[...]) for part in _split_bf16(stack, 2))

    lower = lax.broadcasted_iota(jnp.int32, (q, q), 0) >= lax.broadcasted_iota(jnp.int32, (q, q), 1)
    lane = lax.broadcasted_iota(jnp.int32, (q, 2 * HEAD_DIM), 1)
    for g in range(N_GROUPS):
        gs = slice(g * gw, (g + 1) * gw)
        bm = xc_ref[:, di + g * D_STATE:di + (g + 1) * D_STATE]
        cm = xc_ref[:, di + (N_GROUPS + g) * D_STATE:di + (N_GROUPS + g + 1) * D_STATE]
        bb = bm.astype(BF16)
        cb16 = cm.astype(BF16)
        xs = xc_ref[:, gs]
        cbm = _dot_nt(cb16, bb)
        st = st_ref[g]
        y_off = _dot(cb16, st.astype(BF16)) * ex[0:q, gs]
        xw = (xs * ex[q:2 * q, gs]).astype(BF16)
        st_ref[g] = st * ex[2 * q:2 * q + 1, gs] + _dot(bm.T.astype(BF16), xw)
        for k in range(hg // 2):
            ms = []
            for hh in (2 * k, 2 * k + 1):
                hd = g * hg + hh
                seg2 = cs2[:, hd:hd + 1] - cs2_t[hd:hd + 1, :]
                dec = jnp.exp2(jnp.where(lower, seg2, -jnp.inf))
                ms.append((cbm * dec * dt_t[hd:hd + 1, :]).astype(BF16))
            ks = slice(2 * k * HEAD_DIM, (2 * k + 2) * HEAD_DIM)
            x2 = xs[:, ks]
            rhs = jnp.concatenate([jnp.where(lane < HEAD_DIM, x2, 0.0),
                                   jnp.where(lane >= HEAD_DIM, x2, 0.0)], axis=0).astype(BF16)
            y_diag = _dot(jnp.concatenate(ms, axis=1), rhs)
            ys = slice(g * gw + 2 * k * HEAD_DIM, g * gw + (2 * k + 2) * HEAD_DIM)
            y_ref[:, ys] = y_diag + y_off[:, ks] + x2 * dx_ref[:, ys]

    z = z_ref[0].astype(F32)
    o_ref[0] = _rms(y_ref[...] * _silu(z), ng_ref[...]).astype(BF16)


def _ssd_scan(xbc, dt_raw, z, conv_w, conv_b, dt_bias, a_log, d_skip, norm_g):
    b, l, conv_dim = xbc.shape
    di = z.shape[2]
    n_heads = di // HEAD_DIM
    q = CHUNK
    lanes = lambda v: jnp.broadcast_to(v[:, None], (n_heads, LANES))
    tri = jnp.triu(jnp.ones((q, q), BF16))
    expand = (jnp.arange(LANES)[:, None] == (jnp.arange(di) // HEAD_DIM)[None, :]).astype(BF16)
    args = (xbc, dt_raw, z, conv_w, conv_b.reshape(1, conv_dim), lanes(dt_bias), lanes(a_log),
            jnp.repeat(d_skip, HEAD_DIM).reshape(1, di), norm_g.reshape(1, di), tri, expand)
    return pl.pallas_call(
        _ssd_scan_kernel,
        grid=(b, l // q),
        in_specs=[
            pl.BlockSpec((1, q, conv_dim), lambda i, c: (i, c, 0)),
            pl.BlockSpec((1, q, LANES), lambda i, c: (i, c, 0)),
            pl.BlockSpec((1, q, di), lambda i, c: (i, c, 0)),
        ] + [_const_spec(a.shape) for a in args[3:]],
        out_specs=pl.BlockSpec((1, q, di), lambda i, c: (i, c, 0)),
        out_shape=jax.ShapeDtypeStruct((b, l, di), BF16),
        scratch_shapes=[
            pltpu.VMEM((conv_dim // LANES, 2 * (SUBLANES + q), LANES), F32),
            pltpu.VMEM((N_GROUPS, D_STATE, di // N_GROUPS), F32),
            pltpu.VMEM((q, conv_dim), F32),
            pltpu.VMEM((q, di), F32),
        ],
        compiler_params=_cparams("arbitrary", "arbitrary"),
        name="ssd_scan",
    )(*args)


def _proj_residual_kernel(a_ref, x_ref, g_ref, w_ref, o_ref, *, g_row):
    out = _dot(a_ref[...], w_ref[...])
    o_ref[...] = x_ref[...] + _rms(out, g_ref[g_row:g_row + 1, :])


def _proj_residual(a2d, x2d, g, w, g_row, tm):
    t, k = a2d.shape
    d = x2d.shape[1]
    return pl.pallas_call(
        functools.partial(_proj_residual_kernel, g_row=g_row),
        grid=(t // tm,),
        in_specs=[
            pl.BlockSpec((tm, k), lambda i: (i, 0)),
            pl.BlockSpec((tm, d), lambda i: (i, 0)),
            _const_spec(g.shape),
            _const_spec(w.shape),
        ],
        out_specs=pl.BlockSpec((tm, d), lambda i: (i, 0)),
        out_shape=jax.ShapeDtypeStruct(x2d.shape, F32),
        compiler_params=_cparams("arbitrary"),
        name="proj_residual",
    )(a2d, x2d, g, w)


def _ssd_layer(x, g, in_w, conv_w, conv_b, dt_bias, a_log, d_skip, norm_g, out_w, tm):
    b, l, d = x.shape
    di = out_w.shape[0]
    conv_dim = conv_w.shape[1]
    n_heads = dt_bias.shape[0]
    w_zx = in_w[:, :di + conv_dim].astype(BF16)
    w_dt = jnp.pad(in_w[:, di + conv_dim:], ((0, 0), (0, LANES - n_heads))).astype(BF16)
    z, xbc, dt_raw = _ssd_inproj(x.reshape(b * l, d), g, w_zx, w_dt, di, tm, 512)
    yg = _ssd_scan(xbc.reshape(b, l, conv_dim), dt_raw.reshape(b, l, LANES), z.reshape(b, l, di),
                   conv_w, conv_b, dt_bias, a_log, d_skip, norm_g)
    return _proj_residual(yg.reshape(b * l, di), x.reshape(b * l, d), g, out_w.astype(BF16),
                          1, tm).reshape(b, l, d)


def _tile(n, want):
    t = min(n, want)
    assert n % t == 0, (n, t)
    return t


def kernel(x, mem, norm_g, ssm_in_w, ssm_conv_w, ssm_conv_b, ssm_dt_bias, ssm_A_log, ssm_D, ssm_norm_g, ssm_out_w, cf_pw1_w, cf_pw1_b, cf_dw_w, cf_dw_b, cf_ln_g, cf_ln_b, cf_pw2_w, cf_pw2_b, xa_mem_g, xa_q_w, xa_kv_w, xa_o_w, ffn_in_w, ffn_conv_w, ffn_conv_b, ffn_out_w):
    b, l, d = x.shape
    depth = norm_g.shape[0]
    di = ssm_out_w.shape[1]
    n_heads = ssm_dt_bias.shape[1]
    conv_dim = ssm_conv_w.shape[2]
    assert di == n_heads * HEAD_DIM and conv_dim == di + 2 * N_GROUPS * D_STATE
    assert l % CHUNK == 0 and cf_dw_w.shape[1] - 1 <= CF_HALO
    tm = _tile(l, 512)
    bf = lambda w: w.astype(BF16)

    kv = _memory_kv(mem, xa_mem_g, bf(xa_kv_w))
    for i in range(depth):
        g = norm_g[i]
        j = i // 2
        if i % 2 == 0:
            x = _ssd_layer(x, g, ssm_in_w[j], ssm_conv_w[j], ssm_conv_b[j], ssm_dt_bias[j], ssm_A_log[j],
                           ssm_D[j], ssm_norm_g[j], ssm_out_w[j], tm)
        else:
            x = _conformer(x, g, bf(cf_pw1_w[j]), cf_pw1_b[j], cf_dw_w[j], cf_dw_b[j],
                           cf_ln_g[j], cf_ln_b[j], bf(cf_pw2_w[j]), cf_pw2_b[j], tm)
        x = _xattn(x, kv[i], g, bf(xa_q_w[i]), bf(xa_o_w[i]), tm)
        x = _ffn(x, g, bf(ffn_in_w[i]), ffn_conv_w[i], ffn_conv_b[i], bf(ffn_out_w[i]), tm,
                 ffn_out_w.shape[1])
    return x
```

```python
import functools

import jax
import jax.numpy as jnp
from jax import lax
from jax.experimental import pallas as pl
from jax.experimental.pallas import tpu as pltpu

F32 = jnp.float32
BF16 = jnp.bfloat16

EPS = 1e-6
LOG2E = 1.4426950408889634
HEAD_DIM = 64
N_GROUPS = 4
D_STATE = 128
CHUNK = 128
XA_HEADS = 4
SUBLANES = 8
LANES = 128
SCAN_BATCH = 2
CF_ROWS = 128
CF_HALO = 32
VMEM_LIMIT_BYTES = 56 * 1024 * 1024


def _cparams(*sem):
    return pltpu.CompilerParams(dimension_semantics=sem, vmem_limit_bytes=VMEM_LIMIT_BYTES)


def _const_spec(shape):
    nd = len(shape)
    return pl.BlockSpec(shape, lambda *_: (0,) * nd, pipeline_mode=pl.Buffered(1))


def _rms(xf, g):
    ms = jnp.mean(xf * xf, axis=-1, keepdims=True)
    return xf * lax.rsqrt(ms + EPS) * g


def _dot(a, b):
    return jnp.dot(a, b, preferred_element_type=F32)


def _dot_nt(a, b):
    return lax.dot_general(a, b, (((1,), (1,)), ((), ())), preferred_element_type=F32)


def _sigmoid(x):
    return 0.5 * jnp.tanh(0.5 * x) + 0.5


def _silu(x):
    h = 0.5 * x
    return h + h * jnp.tanh(h)


def _shift_rows(u, prev, s, row8):
    r = pltpu.roll(u, s, 0)
    p = pltpu.roll(prev, s, 0)
    top = jnp.where(row8 < s, p, r[:SUBLANES])
    return jnp.concatenate([top, r[SUBLANES:]], axis=0)


def _split_bf16(x, parts):
    out = []
    r = x
    for i in range(parts):
        p = r.astype(BF16)
        out.append(p)
        if i + 1 < parts:
            r = r - p.astype(F32)
    return out


def _kv_kernel(mem_ref, g_ref, w_ref, o_ref):
    m = _rms(mem_ref[0], g_ref[0]).astype(BF16)
    o_ref[0, 0] = _dot(m, w_ref[0]).astype(BF16)


def _memory_kv(mem, mem_g, kv_w):
    depth, d, d2 = kv_w.shape
    b, n_mem, _ = mem.shape
    return pl.pallas_call(
        _kv_kernel,
        grid=(depth, b),
        in_specs=[
            pl.BlockSpec((1, n_mem, d), lambda l, i: (i, 0, 0)),
            pl.BlockSpec((1, 1, d), lambda l, i: (l, 0, 0)),
            pl.BlockSpec((1, d, d2), lambda l, i: (l, 0, 0)),
        ],
        out_specs=pl.BlockSpec((1, 1, n_mem, d2), lambda l, i: (l, i, 0, 0)),
        out_shape=jax.ShapeDtypeStruct((depth, b, n_mem, d2), BF16),
        compiler_params=_cparams("arbitrary", "arbitrary"),
        name="memory_kv",
    )(mem, mem_g.reshape(depth, 1, d), kv_w)


def _xattn_kernel(x_ref, kv_ref, g_ref, qw_ref, ow_ref, o_ref):
    x = x_ref[0]
    d = x.shape[-1]
    dh = d // XA_HEADS
    h = _rms(x, g_ref[2:3, :]).astype(BF16)
    q = (_dot(h, qw_ref[...]) * (dh ** -0.5)).astype(BF16)
    heads = []
    for hd in range(XA_HEADS):
        qh = q[:, hd * dh:(hd + 1) * dh]
        kh = kv_ref[0, :, hd * dh:(hd + 1) * dh]
        vh = kv_ref[0, :, d + hd * dh:d + (hd + 1) * dh]
        s = _dot_nt(qh, kh)
        p = jnp.exp(s - jnp.max(s, axis=-1, keepdims=True))
        l = jnp.sum(p, axis=-1, keepdims=True)
        heads.append((_dot(p.astype(BF16), vh) / l).astype(BF16))
    o = jnp.concatenate(heads, axis=-1)
    a = _dot(o, ow_ref[...])
    o_ref[0] = x + _rms(a, g_ref[3:4, :])


def _xattn(x, kv, g, q_w, o_w, tm):
    b, l, d = x.shape
    n_mem = kv.shape[1]
    return pl.pallas_call(
        _xattn_kernel,
        grid=(b, l // tm),
        in_specs=[
            pl.BlockSpec((1, tm, d), lambda i, t: (i, t, 0)),
            pl.BlockSpec((1, n_mem, 2 * d), lambda i, t: (i, 0, 0)),
            _const_spec(g.shape),
            _const_spec(q_w.shape),
            _const_spec(o_w.shape),
        ],
        out_specs=pl.BlockSpec((1, tm, d), lambda i, t: (i, t, 0)),
        out_shape=jax.ShapeDtypeStruct(x.shape, F32),
        compiler_params=_cparams("arbitrary", "arbitrary"),
        name="xattn",
    )(x, kv, g, q_w, o_w)


def _ffn_kernel(x_ref, g_ref, inw_ref, cw_ref, cb_ref, outw_ref, o_ref, carry_ref, *, nb):
    @pl.when(pl.program_id(1) == 0)
    def _():
        carry_ref[...] = jnp.zeros_like(carry_ref)

    x = x_ref[0]
    tm = x.shape[0]
    f = outw_ref.shape[0]
    h = _rms(x, g_ref[4:5, :]).astype(BF16)
    row8 = lax.broadcasted_iota(jnp.int32, (SUBLANES, nb), 0)
    acc = jnp.zeros(x.shape, F32)
    for j in range(f // nb):
        halves = []
        for half in range(2):
            c0 = half * f + j * nb
            u = _dot(h, inw_ref[:, c0:c0 + nb])
            prev = carry_ref[:, c0:c0 + nb]
            carry_ref[:, c0:c0 + nb] = u[tm - SUBLANES:, :]
            u1 = _shift_rows(u, prev, 1, row8)
            u2 = _shift_rows(u, prev, 2, row8)
            halves.append(cw_ref[2:3, c0:c0 + nb] * u + cw_ref[1:2, c0:c0 + nb] * u1
                          + cw_ref[0:1, c0:c0 + nb] * u2 + cb_ref[:, c0:c0 + nb])
        gate, val = halves
        act = (_silu(gate) * val).astype(BF16)
        acc = acc + _dot(act, outw_ref[j * nb:(j + 1) * nb, :])
    o_ref[0] = x + _rms(acc, g_ref[5:6, :])


def _ffn(x, g, in_w, conv_w, conv_b, out_w, tm, nb):
    b, l, d = x.shape
    f = out_w.shape[0]
    return pl.pallas_call(
        functools.partial(_ffn_kernel, nb=nb),
        grid=(b, l // tm),
        in_specs=[
            pl.BlockSpec((1, tm, d), lambda i, t: (i, t, 0)),
            _const_spec(g.shape),
            _const_spec(in_w.shape),
            _const_spec(conv_w.shape),
            _const_spec((1, 2 * f)),
            _const_spec(out_w.shape),
        ],
        out_specs=pl.BlockSpec((1, tm, d), lambda i, t: (i, t, 0)),
        out_shape=jax.ShapeDtypeStruct(x.shape, F32),
        scratch_shapes=[pltpu.VMEM((SUBLANES, 2 * f), F32)],
        compiler_params=_cparams("arbitrary", "arbitrary"),
        name="conv_ffn",
    )(x, g, in_w, conv_w, conv_b.reshape(1, 2 * f), out_w)


def _conformer_kernel(x_ref, g_ref, w1_ref, b1_ref, dww_ref, dwb_ref, lng_ref, lnb_ref,
                      w2_ref, b2_ref, o_ref, buf_ref, c_ref):
    nslab = buf_ref.shape[0]

    @pl.when(pl.program_id(1) == 0)
    def _():
        buf_ref[:, 0:2 * CF_HALO, :] = jnp.zeros((nslab, 2 * CF_HALO, LANES), F32)

    x = x_ref[0]
    tm, d = x.shape
    taps = dww_ref.shape[1]
    h = _rms(x, g_ref[0:1, :]).astype(BF16)
    a = _dot(h, w1_ref[:, 0:d]) + b1_ref[:, 0:d]
    gt = _dot(h, w1_ref[:, d:2 * d]) + b1_ref[:, d:2 * d]
    glu = a * _sigmoid(gt)
    for j in range(nslab):
        buf_ref[j, pl.ds(2 * CF_HALO, tm, stride=2), :] = glu[:, j * LANES:(j + 1) * LANES]

    off = CF_HALO - (taps - 1)
    nv = CF_ROWS // SUBLANES
    for r0 in range(0, tm, CF_ROWS):
        for j in range(nslab):
            acc = [jnp.broadcast_to(dwb_ref[j], (SUBLANES, LANES))] * nv
            for q in range(SUBLANES * (nv - 1) + taps):
                win = buf_ref[j, pl.ds(2 * (r0 + off + q), SUBLANES, stride=2), :]
                for v in range(nv):
                    k = q - SUBLANES * v
                    if 0 <= k < taps:
                        acc[v] = acc[v] + dww_ref[j, k:k + 1, :] * win
            c_ref[j, r0:r0 + CF_ROWS, :] = jnp.concatenate(acc, axis=0)
    for j in range(nslab):
        buf_ref[j, pl.ds(0, CF_HALO, stride=2), :] = buf_ref[j, pl.ds(2 * tm, CF_HALO, stride=2), :]

    c = jnp.concatenate([c_ref[j] for j in range(nslab)], axis=1)
    mu = jnp.mean(c, axis=-1, keepdims=True)
    cc = c - mu
    var = jnp.mean(cc * cc, axis=-1, keepdims=True)
    y = cc * lax.rsqrt(var + EPS) * lng_ref[...] + lnb_ref[...]
    y = _silu(y).astype(BF16)
    out = _dot(y, w2_ref[...]) + b2_ref[...]
    o_ref[0] = x + _rms(out, g_ref[1:2, :])


def _conformer(x, g, pw1_w, pw1_b, dw_w, dw_b, ln_g, ln_b, pw2_w, pw2_b, tm):
    b, l, d = x.shape
    taps = dw_w.shape[0]
    nslab = d // LANES
    row = lambda v: v.reshape(1, -1)
    dww = dw_w.reshape(taps, nslab, LANES).transpose(1, 0, 2)
    dwb = dw_b.reshape(nslab, 1, LANES)
    args = (x, g, pw1_w, row(pw1_b), dww, dwb, row(ln_g), row(ln_b), pw2_w, row(pw2_b))
    return pl.pallas_call(
        _conformer_kernel,
        grid=(b, l // tm),
        in_specs=[pl.BlockSpec((1, tm, d), lambda i, t: (i, t, 0))]
        + [_const_spec(a.shape) for a in args[1:]],
        out_specs=pl.BlockSpec((1, tm, d), lambda i, t: (i, t, 0)),
        out_shape=jax.ShapeDtypeStruct(x.shape, F32),
        scratch_shapes=[pltpu.VMEM((nslab, 2 * (CF_HALO + tm), LANES), F32),
                        pltpu.VMEM((nslab, tm, LANES), F32)],
        compiler_params=_cparams("arbitrary", "arbitrary"),
        name="conformer",
    )(*args)


def _ssd_chunk(x_ref, g_ref, wzx_ref, wdt_ref, cw_ref, cb_ref, dtb_ref, alog_ref, dx_ref, ng_ref,
               tri_ref, exp_ref, wo_ref, o_ref, cbuf_ref, st_ref, xc_ref, y_ref):
    x = x_ref[...]
    q = x.shape[0]
    conv_dim = cw_ref.shape[1]
    di = wo_ref.shape[0]
    h = _rms(x, g_ref[0:1, :]).astype(BF16)
    gw = di // N_GROUPS
    hg = gw // HEAD_DIM
    taps = cw_ref.shape[0]

    for jb in range(conv_dim // (2 * LANES)):
        u2 = _dot(h, wzx_ref[:, di + jb * 2 * LANES:di + (jb + 1) * 2 * LANES])
        for jj in range(2):
            j = 2 * jb + jj
            sl = slice(j * LANES, (j + 1) * LANES)
            u = u2[:, jj * LANES:(jj + 1) * LANES]
            cbuf_ref[j, pl.ds(2 * SUBLANES, q, stride=2), :] = u
            c = cw_ref[taps - 1:taps, sl] * u + cb_ref[:, sl]
            for s in range(1, taps):
                c = c + cw_ref[taps - 1 - s:taps - s, sl] * cbuf_ref[j, pl.ds(2 * (SUBLANES - s), q, stride=2), :]
            cbuf_ref[j, pl.ds(0, SUBLANES, stride=2), :] = cbuf_ref[j, pl.ds(2 * q, SUBLANES, stride=2), :]
            xc_ref[:, sl] = _silu(c)

    nh = dtb_ref.shape[0]
    xdt = _dot_nt(wdt_ref[...], h) + dtb_ref[...]
    dt_t = jnp.maximum(xdt, 0.0) + jnp.log1p(jnp.exp(-jnp.abs(xdt)))
    da_t = dt_t * (-jnp.exp(alog_ref[...]))
    cs_t = _dot(jnp.concatenate(_split_bf16(da_t, 3), axis=1), tri_ref[...])
    pad_rows = jnp.zeros((LANES - nh, q), F32)
    cs = jnp.concatenate([cs_t, pad_rows], axis=0).T
    dt = jnp.concatenate([dt_t, pad_rows], axis=0).T
    cs2 = cs * LOG2E
    cs2_t = cs_t * LOG2E
    cs_last = cs[q - 1:q, :]
    stack = jnp.concatenate([
        jnp.exp(cs),
        dt * jnp.exp(cs_last - cs),
        jnp.broadcast_to(jnp.exp(cs_last), (SUBLANES, cs.shape[1])),
    ], axis=0)
    ex = _dot(jnp.concatenate(_split_bf16(stack, 2), axis=1), exp_ref[...])

    lower = lax.broadcasted_iota(jnp.int32, (q, q), 0) >= lax.broadcasted_iota(jnp.int32, (q, q), 1)
    lane = lax.broadcasted_iota(jnp.int32, (q, 2 * HEAD_DIM), 1)
    for g in range(N_GROUPS):
        gs = slice(g * gw, (g + 1) * gw)
        bm = xc_ref[:, di + g * D_STATE:di + (g + 1) * D_STATE]
        cm = xc_ref[:, di + (N_GROUPS + g) * D_STATE:di + (N_GROUPS + g + 1) * D_STATE]
        bb = bm.astype(BF16)
        cb16 = cm.astype(BF16)
        xs = xc_ref[:, gs]
        cbm = _dot_nt(cb16, bb)
        st = st_ref[g]
        y_off = _dot(cb16, st.astype(BF16)) * ex[0:q, gs]
        xw = (xs * ex[q:2 * q, gs]).astype(BF16)
        st_ref[g] = st * ex[2 * q:2 * q + 1, gs] + _dot(bm.T.astype(BF16), xw)
        for k in range(hg // 2):
            ms = []
            for hh in (2 * k, 2 * k + 1):
                hd = g * hg + hh
                seg2 = cs2[:, hd:hd + 1] - cs2_t[hd:hd + 1, :]
                dec = jnp.exp2(jnp.where(lower, seg2, -jnp.inf))
                ms.append((cbm * dec * dt_t[hd:hd + 1, :]).astype(BF16))
            ks = slice(2 * k * HEAD_DIM, (2 * k + 2) * HEAD_DIM)
            x2 = xs[:, ks]
            rhs = jnp.concatenate([jnp.where(lane < HEAD_DIM, x2, 0.0),
                                   jnp.where(lane >= HEAD_DIM, x2, 0.0)], axis=0).astype(BF16)
            y_diag = _dot(jnp.concatenate(ms, axis=1), rhs)
            ys = slice(g * gw + 2 * k * HEAD_DIM, g * gw + (2 * k + 2) * HEAD_DIM)
            y_ref[:, ys] = y_diag + y_off[:, ks] + x2 * dx_ref[:, ys]

    z = _dot(h, wzx_ref[:, 0:di])
    yg = _rms(y_ref[...] * _silu(z), ng_ref[...]).astype(BF16)
    o_ref[...] = x + _rms(_dot(yg, wo_ref[...]), g_ref[1:2, :])


def _ssd_kernel(x_ref, g_ref, wzx_ref, wdt_ref, cw_ref, cb_ref, dtb_ref, alog_ref, dx_ref, ng_ref,
                tri_ref, exp_ref, wo_ref, o_ref, cbuf_ref, st_ref, xc_ref, y_ref):
    @pl.when(pl.program_id(1) == 0)
    def _():
        cbuf_ref[:, :, 0:2 * SUBLANES, :] = jnp.zeros(
            (cbuf_ref.shape[0], cbuf_ref.shape[1], 2 * SUBLANES, LANES), F32)
        st_ref[...] = jnp.zeros_like(st_ref)

    for bi in range(x_ref.shape[0]):
        _ssd_chunk(x_ref.at[bi], g_ref, wzx_ref, wdt_ref, cw_ref, cb_ref, dtb_ref, alog_ref, dx_ref, ng_ref,
                   tri_ref, exp_ref, wo_ref, o_ref.at[bi], cbuf_ref.at[bi], st_ref.at[bi], xc_ref.at[bi],
                   y_ref.at[bi])


def _ssd_layer(x, g, in_w, conv_w, conv_b, dt_bias, a_log, d_skip, norm_g, out_w):
    b, l, d = x.shape
    di = out_w.shape[0]
    conv_dim = conv_w.shape[1]
    n_heads = dt_bias.shape[0]
    q = CHUNK
    nb = SCAN_BATCH
    w_zx = in_w[:, :di + conv_dim].astype(BF16)
    w_dt = in_w[:, di + conv_dim:].T.astype(BF16)
    lanes = lambda v: jnp.broadcast_to(v[:, None], (n_heads, LANES))
    tri = jnp.tile(jnp.triu(jnp.ones((q, q), BF16)), (3, 1))
    expand = jnp.tile((jnp.arange(LANES)[:, None] == (jnp.arange(di) // HEAD_DIM)[None, :]).astype(BF16), (2, 1))
    args = (x, g, w_zx, w_dt, conv_w, conv_b.reshape(1, conv_dim), lanes(dt_bias), lanes(a_log),
            jnp.repeat(d_skip, HEAD_DIM).reshape(1, di), norm_g.reshape(1, di), tri, expand,
            out_w.astype(BF16))
    return pl.pallas_call(
        _ssd_kernel,
        grid=(b // nb, l // q),
        in_specs=[pl.BlockSpec((nb, q, d), lambda i, c: (i, c, 0))] + [_const_spec(a.shape) for a in args[1:]],
        out_specs=pl.BlockSpec((nb, q, d), lambda i, c: (i, c, 0)),
        out_shape=jax.ShapeDtypeStruct(x.shape, F32),
        scratch_shapes=[
            pltpu.VMEM((nb, conv_dim // LANES, 2 * (SUBLANES + q), LANES), F32),
            pltpu.VMEM((nb, N_GROUPS, D_STATE, di // N_GROUPS), F32),
            pltpu.VMEM((nb, q, conv_dim), F32),
            pltpu.VMEM((nb, q, di), F32),
        ],
        compiler_params=_cparams("arbitrary", "arbitrary"),
        name="ssd_layer",
    )(*args)


def _tile(n, want):
    t = min(n, want)
    assert n % t == 0, (n, t)
    return t


def kernel(x, mem, norm_g, ssm_in_w, ssm_conv_w, ssm_conv_b, ssm_dt_bias, ssm_A_log, ssm_D, ssm_norm_g, ssm_out_w, cf_pw1_w, cf_pw1_b, cf_dw_w, cf_dw_b, cf_ln_g, cf_ln_b, cf_pw2_w, cf_pw2_b, xa_mem_g, xa_q_w, xa_kv_w, xa_o_w, ffn_in_w, ffn_conv_w, ffn_conv_b, ffn_out_w):
    b, l, d = x.shape
    depth = norm_g.shape[0]
    di = ssm_out_w.shape[1]
    n_heads = ssm_dt_bias.shape[1]
    conv_dim = ssm_conv_w.shape[2]
    assert di == n_heads * HEAD_DIM and conv_dim == di + 2 * N_GROUPS * D_STATE
    assert l % CHUNK == 0 and b % SCAN_BATCH == 0 and cf_dw_w.shape[1] - 1 <= CF_HALO
    tm = _tile(l, 512)
    bf = lambda w: w.astype(BF16)

    kv = _memory_kv(mem, xa_mem_g, bf(xa_kv_w))
    for i in range(depth):
        g = norm_g[i]
        j = i // 2
        if i % 2 == 0:
            x = _ssd_layer(x, g, ssm_in_w[j], ssm_conv_w[j], ssm_conv_b[j], ssm_dt_bias[j], ssm_A_log[j],
                           ssm_D[j], ssm_norm_g[j], ssm_out_w[j])
        else:
            x = _conformer(x, g, bf(cf_pw1_w[j]), cf_pw1_b[j], cf_dw_w[j], cf_dw_b[j],
                           cf_ln_g[j], cf_ln_b[j], bf(cf_pw2_w[j]), cf_pw2_b[j], tm)
        x = _xattn(x, kv[i], g, bf(xa_q_w[i]), bf(xa_o_w[i]), tm)
        x = _ffn(x, g, bf(ffn_in_w[i]), ffn_conv_w[i], ffn_conv_b[i], bf(ffn_out_w[i]), tm,
                 ffn_out_w.shape[1])
    return x
```

```python
import functools

import jax
import jax.numpy as jnp
from jax import lax
from jax.experimental import pallas as pl
from jax.experimental.pallas import tpu as pltpu

F32 = jnp.float32
BF16 = jnp.bfloat16

EPS = 1e-6
LOG2E = 1.4426950408889634
HEAD_DIM = 64
N_GROUPS = 4
D_STATE = 128
CHUNK = 128
XA_HEADS = 4
SUBLANES = 8
LANES = 128
SCAN_BATCH = 4
CF_ROWS = 128
CF_HALO = 32
VMEM_LIMIT_BYTES = 56 * 1024 * 1024


def _cparams(*sem):
    return pltpu.CompilerParams(dimension_semantics=sem, vmem_limit_bytes=VMEM_LIMIT_BYTES)


def _const_spec(shape):
    nd = len(shape)
    return pl.BlockSpec(shape, lambda *_: (0,) * nd, pipeline_mode=pl.Buffered(1))


def _rms(xf, g):
    ms = jnp.mean(xf * xf, axis=-1, keepdims=True)
    return xf * lax.rsqrt(ms + EPS) * g


def _dot(a, b):
    return jnp.dot(a, b, preferred_element_type=F32)


def _dot_nt(a, b):
    return lax.dot_general(a, b, (((1,), (1,)), ((), ())), preferred_element_type=F32)


def _sigmoid(x):
    return 0.5 * jnp.tanh(0.5 * x) + 0.5


def _silu(x):
    h = 0.5 * x
    return h + h * jnp.tanh(h)


def _shift_rows(u, prev, s, row8):
    r = pltpu.roll(u, s, 0)
    p = pltpu.roll(prev, s, 0)
    top = jnp.where(row8 < s, p, r[:SUBLANES])
    return jnp.concatenate([top, r[SUBLANES:]], axis=0)


def _split_bf16(x, parts):
    out = []
    r = x
    for i in range(parts):
        p = r.astype(BF16)
        out.append(p)
        if i + 1 < parts:
            r = r - p.astype(F32)
    return out


def _kv_kernel(mem_ref, g_ref, w_ref, o_ref):
    m = _rms(mem_ref[0], g_ref[0]).astype(BF16)
    o_ref[0, 0] = _dot(m, w_ref[0]).astype(BF16)


def _memory_kv(mem, mem_g, kv_w):
    depth, d, d2 = kv_w.shape
    b, n_mem, _ = mem.shape
    return pl.pallas_call(
        _kv_kernel,
        grid=(depth, b),
        in_specs=[
            pl.BlockSpec((1, n_mem, d), lambda l, i: (i, 0, 0)),
            pl.BlockSpec((1, 1, d), lambda l, i: (l, 0, 0)),
            pl.BlockSpec((1, d, d2), lambda l, i: (l, 0, 0)),
        ],
        out_specs=pl.BlockSpec((1, 1, n_mem, d2), lambda l, i: (l, i, 0, 0)),
        out_shape=jax.ShapeDtypeStruct((depth, b, n_mem, d2), BF16),
        compiler_params=_cparams("arbitrary", "arbitrary"),
        name="memory_kv",
    )(mem, mem_g.reshape(depth, 1, d), kv_w)


def _xattn_kernel(x_ref, kv_ref, g_ref, qw_ref, ow_ref, o_ref, *, sub):
    tm, d = x_ref.shape[1], x_ref.shape[2]
    dh = d // XA_HEADS
    n = tm // sub
    xs = [x_ref[0, r * sub:(r + 1) * sub, :] for r in range(n)]
    hs = [_rms(x, g_ref[2:3, :]).astype(BF16) for x in xs]
    qs = [(_dot(h, qw_ref[...]) * (dh ** -0.5)).astype(BF16) for h in hs]
    outs = []
    for q in qs:
        heads = []
        for hd in range(XA_HEADS):
            qh = q[:, hd * dh:(hd + 1) * dh]
            kh = kv_ref[0, :, hd * dh:(hd + 1) * dh]
            vh = kv_ref[0, :, d + hd * dh:d + (hd + 1) * dh]
            s = _dot_nt(qh, kh)
            p = jnp.exp(s - jnp.max(s, axis=-1, keepdims=True))
            l = jnp.sum(p, axis=-1, keepdims=True)
            heads.append((_dot(p.astype(BF16), vh) / l).astype(BF16))
        outs.append(_dot(jnp.concatenate(heads, axis=-1), ow_ref[...]))
    for r in range(n):
        o_ref[0, r * sub:(r + 1) * sub, :] = xs[r] + _rms(outs[r], g_ref[3:4, :])


def _xattn(x, kv, g, q_w, o_w, tm, sub):
    b, l, d = x.shape
    n_mem = kv.shape[1]
    return pl.pallas_call(
        functools.partial(_xattn_kernel, sub=sub),
        grid=(b, l // tm),
        in_specs=[
            pl.BlockSpec((1, tm, d), lambda i, t: (i, t, 0)),
            pl.BlockSpec((1, n_mem, 2 * d), lambda i, t: (i, 0, 0)),
            _const_spec(g.shape),
            _const_spec(q_w.shape),
            _const_spec(o_w.shape),
        ],
        out_specs=pl.BlockSpec((1, tm, d), lambda i, t: (i, t, 0)),
        out_shape=jax.ShapeDtypeStruct(x.shape, F32),
        compiler_params=_cparams("arbitrary", "arbitrary"),
        name="xattn",
    )(x, kv, g, q_w, o_w)


def _ffn_kernel(x_ref, g_ref, inw_ref, cw_ref, cb_ref, outw_ref, o_ref, carry_ref, *, nb):
    @pl.when(pl.program_id(1) == 0)
    def _():
        carry_ref[...] = jnp.zeros_like(carry_ref)

    x = x_ref[0]
    tm = x.shape[0]
    f = outw_ref.shape[0]
    h = _rms(x, g_ref[4:5, :]).astype(BF16)
    row8 = lax.broadcasted_iota(jnp.int32, (SUBLANES, nb), 0)
    acc = jnp.zeros(x.shape, F32)
    for j in range(f // nb):
        halves = []
        for half in range(2):
            c0 = half * f + j * nb
            u = _dot(h, inw_ref[:, c0:c0 + nb])
            prev = carry_ref[:, c0:c0 + nb]
            carry_ref[:, c0:c0 + nb] = u[tm - SUBLANES:, :]
            u1 = _shift_rows(u, prev, 1, row8)
            u2 = _shift_rows(u, prev, 2, row8)
            halves.append(cw_ref[2:3, c0:c0 + nb] * u + cw_ref[1:2, c0:c0 + nb] * u1
                          + cw_ref[0:1, c0:c0 + nb] * u2 + cb_ref[:, c0:c0 + nb])
        gate, val = halves
        act = (_silu(gate) * val).astype(BF16)
        acc = acc + _dot(act, outw_ref[j * nb:(j + 1) * nb, :])
    o_ref[0] = x + _rms(acc, g_ref[5:6, :])


def _ffn(x, g, in_w, conv_w, conv_b, out_w, tm, nb):
    b, l, d = x.shape
    f = out_w.shape[0]
    return pl.pallas_call(
        functools.partial(_ffn_kernel, nb=nb),
        grid=(b, l // tm),
        in_specs=[
            pl.BlockSpec((1, tm, d), lambda i, t: (i, t, 0)),
            _const_spec(g.shape),
            _const_spec(in_w.shape),
            _const_spec(conv_w.shape),
            _const_spec((1, 2 * f)),
            _const_spec(out_w.shape),
        ],
        out_specs=pl.BlockSpec((1, tm, d), lambda i, t: (i, t, 0)),
        out_shape=jax.ShapeDtypeStruct(x.shape, F32),
        scratch_shapes=[pltpu.VMEM((SUBLANES, 2 * f), F32)],
        compiler_params=_cparams("arbitrary", "arbitrary"),
        name="conv_ffn",
    )(x, g, in_w, conv_w, conv_b.reshape(1, 2 * f), out_w)


def _conformer_kernel(x_ref, g_ref, w1_ref, b1_ref, dww_ref, dwb_ref, lng_ref, lnb_ref,
                      w2_ref, b2_ref, o_ref, buf_ref, c_ref):
    nslab = buf_ref.shape[0]

    @pl.when(pl.program_id(1) == 0)
    def _():
        buf_ref[:, 0:2 * CF_HALO, :] = jnp.zeros((nslab, 2 * CF_HALO, LANES), F32)

    x = x_ref[0]
    tm, d = x.shape
    taps = dww_ref.shape[1]
    h = _rms(x, g_ref[0:1, :]).astype(BF16)
    a = _dot(h, w1_ref[:, 0:d]) + b1_ref[:, 0:d]
    gt = _dot(h, w1_ref[:, d:2 * d]) + b1_ref[:, d:2 * d]
    glu = a * _sigmoid(gt)
    for j in range(nslab):
        buf_ref[j, pl.ds(2 * CF_HALO, tm, stride=2), :] = glu[:, j * LANES:(j + 1) * LANES]

    off = CF_HALO - (taps - 1)
    nv = CF_ROWS // SUBLANES
    for r0 in range(0, tm, CF_ROWS):
        for j in range(nslab):
            acc = [jnp.broadcast_to(dwb_ref[j], (SUBLANES, LANES))] * nv
            for q in range(SUBLANES * (nv - 1) + taps):
                win = buf_ref[j, pl.ds(2 * (r0 + off + q), SUBLANES, stride=2), :]
                for v in range(nv):
                    k = q - SUBLANES * v
                    if 0 <= k < taps:
                        acc[v] = acc[v] + dww_ref[j, k:k + 1, :] * win
            c_ref[j, r0:r0 + CF_ROWS, :] = jnp.concatenate(acc, axis=0)
    for j in range(nslab):
        buf_ref[j, pl.ds(0, CF_HALO, stride=2), :] = buf_ref[j, pl.ds(2 * tm, CF_HALO, stride=2), :]

    c = jnp.concatenate([c_ref[j] for j in range(nslab)], axis=1)
    mu = jnp.mean(c, axis=-1, keepdims=True)
    cc = c - mu
    var = jnp.mean(cc * cc, axis=-1, keepdims=True)
    y = cc * lax.rsqrt(var + EPS) * lng_ref[...] + lnb_ref[...]
    y = _silu(y).astype(BF16)
    out = _dot(y, w2_ref[...]) + b2_ref[...]
    o_ref[0] = x + _rms(out, g_ref[1:2, :])


def _conformer(x, g, pw1_w, pw1_b, dw_w, dw_b, ln_g, ln_b, pw2_w, pw2_b, tm):
    b, l, d = x.shape
    taps = dw_w.shape[0]
    nslab = d // LANES
    row = lambda v: v.reshape(1, -1)
    dww = dw_w.reshape(taps, nslab, LANES).transpose(1, 0, 2)
    dwb = dw_b.reshape(nslab, 1, LANES)
    args = (x, g, pw1_w, row(pw1_b), dww, dwb, row(ln_g), row(ln_b), pw2_w, row(pw2_b))
    return pl.pallas_call(
        _conformer_kernel,
        grid=(b, l // tm),
        in_specs=[pl.BlockSpec((1, tm, d), lambda i, t: (i, t, 0))]
        + [_const_spec(a.shape) for a in args[1:]],
        out_specs=pl.BlockSpec((1, tm, d), lambda i, t: (i, t, 0)),
        out_shape=jax.ShapeDtypeStruct(x.shape, F32),
        scratch_shapes=[pltpu.VMEM((nslab, 2 * (CF_HALO + tm), LANES), F32),
                        pltpu.VMEM((nslab, tm, LANES), F32)],
        compiler_params=_cparams("arbitrary", "arbitrary"),
        name="conformer",
    )(*args)


def _ssd_chunk(x_ref, g_ref, wzx_ref, wdt_ref, cw_ref, cb_ref, dtb_ref, alog_ref, dx_ref, ng_ref,
               tri_ref, exp_ref, wo_ref, o_ref, cbuf_ref, st_ref, xc_ref, y_ref):
    x = x_ref[...]
    q = x.shape[0]
    conv_dim = cw_ref.shape[1]
    di = wo_ref.shape[0]
    h = _rms(x, g_ref[0:1, :]).astype(BF16)
    gw = di // N_GROUPS
    hg = gw // HEAD_DIM
    taps = cw_ref.shape[0]

    for jb in range(conv_dim // (2 * LANES)):
        u2 = _dot(h, wzx_ref[:, di + jb * 2 * LANES:di + (jb + 1) * 2 * LANES])
        for jj in range(2):
            j = 2 * jb + jj
            sl = slice(j * LANES, (j + 1) * LANES)
            u = u2[:, jj * LANES:(jj + 1) * LANES]
            cbuf_ref[j, pl.ds(2 * SUBLANES, q, stride=2), :] = u
            c = cw_ref[taps - 1:taps, sl] * u + cb_ref[:, sl]
            for s in range(1, taps):
                c = c + cw_ref[taps - 1 - s:taps - s, sl] * cbuf_ref[j, pl.ds(2 * (SUBLANES - s), q, stride=2), :]
            cbuf_ref[j, pl.ds(0, SUBLANES, stride=2), :] = cbuf_ref[j, pl.ds(2 * q, SUBLANES, stride=2), :]
            xc_ref[:, sl] = _silu(c)

    nh = dtb_ref.shape[0]
    xdt = _dot_nt(wdt_ref[...], h) + dtb_ref[...]
    dt_t = jnp.maximum(xdt, 0.0) + jnp.log1p(jnp.exp(-jnp.abs(xdt)))
    da_t = dt_t * (-jnp.exp(alog_ref[...]))
    cs_t = _dot(jnp.concatenate(_split_bf16(da_t, 3), axis=1), tri_ref[...])
    pad_rows = jnp.zeros((LANES - nh, q), F32)
    cs = jnp.concatenate([cs_t, pad_rows], axis=0).T
    dt = jnp.concatenate([dt_t, pad_rows], axis=0).T
    cs2 = cs * LOG2E
    cs2_t = cs_t * LOG2E
    cs_last = cs[q - 1:q, :]
    stack = jnp.concatenate([
        jnp.exp(cs),
        dt * jnp.exp(cs_last - cs),
        jnp.broadcast_to(jnp.exp(cs_last), (SUBLANES, cs.shape[1])),
    ], axis=0)
    ex = _dot(jnp.concatenate(_split_bf16(stack, 2), axis=1), exp_ref[...])

    lower = lax.broadcasted_iota(jnp.int32, (q, q), 0) >= lax.broadcasted_iota(jnp.int32, (q, q), 1)
    lane = lax.broadcasted_iota(jnp.int32, (q, 2 * HEAD_DIM), 1)
    for g in range(N_GROUPS):
        gs = slice(g * gw, (g + 1) * gw)
        bm = xc_ref[:, di + g * D_STATE:di + (g + 1) * D_STATE]
        cm = xc_ref[:, di + (N_GROUPS + g) * D_STATE:di + (N_GROUPS + g + 1) * D_STATE]
        bb = bm.astype(BF16)
        cb16 = cm.astype(BF16)
        xs = xc_ref[:, gs]
        cbm = _dot_nt(cb16, bb)
        st = st_ref[g]
        y_off = _dot(cb16, st.astype(BF16)) * ex[0:q, gs]
        xw = (xs * ex[q:2 * q, gs]).astype(BF16)
        st_ref[g] = st * ex[2 * q:2 * q + 1, gs] + _dot(bm.T.astype(BF16), xw)
        for k in range(hg // 2):
            ms = []
            for hh in (2 * k, 2 * k + 1):
                hd = g * hg + hh
                seg2 = cs2[:, hd:hd + 1] - cs2_t[hd:hd + 1, :]
                dec = jnp.exp2(jnp.where(lower, seg2, -jnp.inf))
                ms.append((cbm * dec * dt_t[hd:hd + 1, :]).astype(BF16))
            ks = slice(2 * k * HEAD_DIM, (2 * k + 2) * HEAD_DIM)
            x2 = xs[:, ks]
            rhs = jnp.concatenate([jnp.where(lane < HEAD_DIM, x2, 0.0),
                                   jnp.where(lane >= HEAD_DIM, x2, 0.0)], axis=0).astype(BF16)
            y_diag = _dot(jnp.concatenate(ms, axis=1), rhs)
            ys = slice(g * gw + 2 * k * HEAD_DIM, g * gw + (2 * k + 2) * HEAD_DIM)
            y_ref[:, ys] = y_diag + y_off[:, ks] + x2 * dx_ref[:, ys]

    z = _dot(h, wzx_ref[:, 0:di])
    yg = _rms(y_ref[...] * _silu(z), ng_ref[...]).astype(BF16)
    o_ref[...] = x + _rms(_dot(yg, wo_ref[...]), g_ref[1:2, :])


def _ssd_kernel(x_ref, g_ref, wzx_ref, wdt_ref, cw_ref, cb_ref, dtb_ref, alog_ref, dx_ref, ng_ref,
                tri_ref, exp_ref, wo_ref, o_ref, cbuf_ref, st_ref, xc_ref, y_ref):
    @pl.when(pl.program_id(1) == 0)
    def _():
        cbuf_ref[:, :, 0:2 * SUBLANES, :] = jnp.zeros(
            (cbuf_ref.shape[0], cbuf_ref.shape[1], 2 * SUBLANES, LANES), F32)
        st_ref[...] = jnp.zeros_like(st_ref)

    for bi in range(x_ref.shape[0]):
        _ssd_chunk(x_ref.at[bi], g_ref, wzx_ref, wdt_ref, cw_ref, cb_ref, dtb_ref, alog_ref, dx_ref, ng_ref,
                   tri_ref, exp_ref, wo_ref, o_ref.at[bi], cbuf_ref.at[bi], st_ref.at[bi], xc_ref.at[bi],
                   y_ref.at[bi])


def _ssd_layer(x, g, in_w, conv_w, conv_b, dt_bias, a_log, d_skip, norm_g, out_w):
    b, l, d = x.shape
    di = out_w.shape[0]
    conv_dim = conv_w.shape[1]
    n_heads = dt_bias.shape[0]
    q = CHUNK
    nb = SCAN_BATCH
    w_zx = in_w[:, :di + conv_dim].astype(BF16)
    w_dt = in_w[:, di + conv_dim:].T.astype(BF16)
    lanes = lambda v: jnp.broadcast_to(v[:, None], (n_heads, LANES))
    tri = jnp.tile(jnp.triu(jnp.ones((q, q), BF16)), (3, 1))
    expand = jnp.tile((jnp.arange(LANES)[:, None] == (jnp.arange(di) // HEAD_DIM)[None, :]).astype(BF16), (2, 1))
    args = (x, g, w_zx, w_dt, conv_w, conv_b.reshape(1, conv_dim), lanes(dt_bias), lanes(a_log),
            jnp.repeat(d_skip, HEAD_DIM).reshape(1, di), norm_g.reshape(1, di), tri, expand,
            out_w.astype(BF16))
    return pl.pallas_call(
        _ssd_kernel,
        grid=(b // nb, l // q),
        in_specs=[pl.BlockSpec((nb, q, d), lambda i, c: (i, c, 0))] + [_const_spec(a.shape) for a in args[1:]],
        out_specs=pl.BlockSpec((nb, q, d), lambda i, c: (i, c, 0)),
        out_shape=jax.ShapeDtypeStruct(x.shape, F32),
        scratch_shapes=[
            pltpu.VMEM((nb, conv_dim // LANES, 2 * (SUBLANES + q), LANES), F32),
            pltpu.VMEM((nb, N_GROUPS, D_STATE, di // N_GROUPS), F32),
            pltpu.VMEM((nb, q, conv_dim), F32),
            pltpu.VMEM((nb, q, di), F32),
        ],
        compiler_params=_cparams("arbitrary", "arbitrary"),
        name="ssd_layer",
    )(*args)


def _tile(n, want):
    t = min(n, want)
    assert n % t == 0, (n, t)
    return t


def kernel(x, mem, norm_g, ssm_in_w, ssm_conv_w, ssm_conv_b, ssm_dt_bias, ssm_A_log, ssm_D, ssm_norm_g, ssm_out_w, cf_pw1_w, cf_pw1_b, cf_dw_w, cf_dw_b, cf_ln_g, cf_ln_b, cf_pw2_w, cf_pw2_b, xa_mem_g, xa_q_w, xa_kv_w, xa_o_w, ffn_in_w, ffn_conv_w, ffn_conv_b, ffn_out_w):
    b, l, d = x.shape
    depth = norm_g.shape[0]
    di = ssm_out_w.shape[1]
    n_heads = ssm_dt_bias.shape[1]
    conv_dim = ssm_conv_w.shape[2]
    assert di == n_heads * HEAD_DIM and conv_dim == di + 2 * N_GROUPS * D_STATE
    assert l % CHUNK == 0 and b % SCAN_BATCH == 0 and cf_dw_w.shape[1] - 1 <= CF_HALO
    tm = _tile(l, 512)
    bf = lambda w: w.astype(BF16)

    kv = _memory_kv(mem, xa_mem_g, bf(xa_kv_w))
    for i in range(depth):
        g = norm_g[i]
        j = i // 2
        if i % 2 == 0:
            x = _ssd_layer(x, g, ssm_in_w[j], ssm_conv_w[j], ssm_conv_b[j], ssm_dt_bias[j], ssm_A_log[j],
                           ssm_D[j], ssm_norm_g[j], ssm_out_w[j])
        else:
            x = _conformer(x, g, bf(cf_pw1_w[j]), cf_pw1_b[j], cf_dw_w[j], cf_dw_b[j],
                           cf_ln_g[j], cf_ln_b[j], bf(cf_pw2_w[j]), cf_pw2_b[j], tm)
        x = _xattn(x, kv[i], g, bf(xa_q_w[i]), bf(xa_o_w[i]), _tile(l, 4 * tm), tm)
        x = _ffn(x, g, bf(ffn_in_w[i]), ffn_conv_w[i], ffn_conv_b[i], bf(ffn_out_w[i]), tm,
                 ffn_out_w.shape[1])
    return x
```
